```python
import jax, jax.numpy as jnp
from jax import lax
import numpy as np

D_MODEL = 4096
BATCH = 4
SEQ = 2048
DEPTH = 2
DEC_BATCH = 8
DEC_SEQ = 8
PAST_LEN = 16384
PAGE_SIZE = 128

SB_HEADS = 16
SB_HEAD_DIM = D_MODEL // 32
SB_WIDTH = SB_HEADS * SB_HEAD_DIM
SB_BLOCK = 128
SB_BIAS_INIT = -8.0
CV_CH = D_MODEL // 4
CV_WIDTH = 31
POOL_CH = D_MODEL // 4
POOL_WINDOWS = (2, 4, 8, 16)
POOL_GROUPS = len(POOL_WINDOWS)
POOL_GC = POOL_CH // POOL_GROUPS
POOL_BUF = max(POOL_WINDOWS) - 1
N_BRANCH = 3
N_IN = 3 * SB_WIDTH + 2 * CV_CH + POOL_CH + N_BRANCH * D_MODEL
N_MEM = 256
XA_HEADS = 4
XA_HEAD_DIM = D_MODEL // 16
XA_WIDTH = XA_HEADS * XA_HEAD_DIM
D_FF = 256 * ((8 * D_MODEL // 3 + 255) // 256)
FFN_CONV = 3
EPS = 1e-6

kernel_name = 'hybrid_stickbreak_conformer_pool_decoder_step'


def rms_norm(x, g):
    x32 = x.astype(jnp.float32)
    y = x32 * lax.rsqrt(jnp.mean(x32 * x32, axis=-1, keepdims=True) + EPS)
    return (y * g.astype(jnp.float32)).astype(x.dtype)


def layer_norm(x, g, b):
    x32 = x.astype(jnp.float32)
    xc = x32 - jnp.mean(x32, axis=-1, keepdims=True)
    y = xc * lax.rsqrt(jnp.mean(xc * xc, axis=-1, keepdims=True) + EPS)
    return (y * g.astype(jnp.float32) + b.astype(jnp.float32)).astype(x.dtype)


def causal_dwconv(x_ext, w, b):
    y = lax.conv_general_dilated(
        x_ext, w[:, None, :].astype(x_ext.dtype), window_strides=(1,), padding='VALID',
        dimension_numbers=('NWC', 'WIO', 'NWC'), feature_group_count=x_ext.shape[-1])
    return y + b.astype(x_ext.dtype)


def stick_breaking(q, k_all, v_all, n_past, bias):
    t = q.shape[1]
    scale = SB_HEAD_DIM ** -0.5
    b32 = bias.astype(jnp.float32)[None, :, None, None]
    outs = []
    for start in range(0, t, SB_BLOCK):
        stop = min(start + SB_BLOCK, t)
        n_keys = n_past + stop
        kb = k_all[:, :n_keys]
        vb = v_all[:, :n_keys]
        z = jnp.einsum('bqhd,bkhd->bhqk', q[:, start:stop], kb).astype(jnp.float32) * scale + b32
        q_pos = n_past + jnp.arange(start, stop)
        k_pos = jnp.arange(n_keys)
        visible = k_pos[None, :] < q_pos[:, None]
        log_fail = jnp.where(visible, jax.nn.log_sigmoid(-z), 0.0)
        log_rest = lax.cumsum(log_fail, axis=3, reverse=True) - log_fail
        weight = jnp.where(visible, jnp.exp(jax.nn.log_sigmoid(z) + log_rest), 0.0)
        outs.append(jnp.einsum('bhqk,bkhd->bqhd', weight.astype(vb.dtype), vb))
    return jnp.concatenate(outs, axis=1)


def multiscale_pool(p_ext, pos):
    t = pos.shape[0]
    p32 = p_ext.astype(jnp.float32)
    csum = jnp.concatenate([jnp.zeros_like(p32[:, :1]), jnp.cumsum(p32, axis=1)], axis=1)
    hi = csum[:, 1 + POOL_BUF:]
    means = []
    for g, w in enumerate(POOL_WINDOWS):
        cs = slice(g * POOL_GC, (g + 1) * POOL_GC)
        lo = csum[:, 1 + POOL_BUF - w:1 + POOL_BUF - w + t, cs]
        count = jnp.minimum(pos + 1, w).astype(jnp.float32)[None, :, None]
        means.append((hi[:, :, cs] - lo) / count)
    pooled = jnp.concatenate(means, axis=-1)
    return (pooled - p32[:, POOL_BUF:]).astype(p_ext.dtype)


def parallel_mixers(h, pos, past_k, past_v, conv_prev, pool_prev,
                    w_in, sb_bias, w_sb_out, cv_dw_w, cv_dw_b, cv_ln_g, cv_ln_b, w_cv_out,
                    w_pool_grp, pool_scale, w_pool_out, w_mix_out):
    b, t, _ = h.shape
    sizes = [SB_WIDTH, SB_WIDTH, SB_WIDTH, CV_CH, CV_CH, POOL_CH, N_BRANCH * D_MODEL]
    offsets = [int(o) for o in np.cumsum(sizes)[:-1]]
    q, k, v, cv_a, cv_b, pool_in, gates = jnp.split(h @ w_in, offsets, axis=-1)
    q = q.reshape(b, t, SB_HEADS, SB_HEAD_DIM)
    k = k.reshape(b, t, SB_HEADS, SB_HEAD_DIM)
    v = v.reshape(b, t, SB_HEADS, SB_HEAD_DIM)
    if past_k is None:
        k_all, v_all, n_past = k, v, 0
    else:
        k_all = jnp.concatenate([past_k.astype(k.dtype), k], axis=1)
        v_all = jnp.concatenate([past_v.astype(v.dtype), v], axis=1)
        n_past = past_k.shape[1]
    y_sb = stick_breaking(q, k_all, v_all, n_past, sb_bias).reshape(b, t, SB_WIDTH) @ w_sb_out
    glu = cv_a * jax.nn.sigmoid(cv_b)
    conv_ext = jnp.concatenate([conv_prev.astype(glu.dtype), glu], axis=1)
    c = jax.nn.silu(layer_norm(causal_dwconv(conv_ext, cv_dw_w, cv_dw_b), cv_ln_g, cv_ln_b))
    y_cv = c @ w_cv_out
    pool_ext = jnp.concatenate([pool_prev.astype(pool_in.dtype), pool_in], axis=1)
    pm = multiscale_pool(pool_ext, pos).reshape(b, t, POOL_GROUPS, POOL_GC)
    pm = jnp.einsum('btgc,gcd->btgd', pm, w_pool_grp).reshape(b, t, POOL_CH) * pool_scale
    y_pool = pm @ w_pool_out
    g_sb, g_cv, g_pool = jnp.split(jax.nn.sigmoid(gates), N_BRANCH, axis=-1)
    out = (g_sb * y_sb + g_cv * y_cv + g_pool * y_pool) @ w_mix_out
    return out, k, v, conv_ext[:, -(CV_WIDTH - 1):], pool_ext[:, -POOL_BUF:]


def memory_kv(mem, g_mem, w_xkv):
    b, m, _ = mem.shape
    mk, mv = jnp.split(rms_norm(mem, g_mem) @ w_xkv, 2, axis=-1)
    return (mk.reshape(b, m, XA_HEADS, XA_HEAD_DIM), mv.reshape(b, m, XA_HEADS, XA_HEAD_DIM))


def cross_attend(h, mem_k, mem_v, w_xq, w_xo):
    b, t, _ = h.shape
    q = (h @ w_xq).reshape(b, t, XA_HEADS, XA_HEAD_DIM)
    s = jnp.einsum('bqhd,bmhd->bhqm', q, mem_k.astype(q.dtype)).astype(jnp.float32) * XA_HEAD_DIM ** -0.5
    a = jax.nn.softmax(s, axis=-1).astype(q.dtype)
    o = jnp.einsum('bhqm,bmhd->bqhd', a, mem_v.astype(q.dtype)).reshape(b, t, XA_WIDTH)
    return o @ w_xo


def conv_ffn(h, ffn_prev, w_up, ffn_dw_w, ffn_dw_b, w_down):
    u = h @ w_up
    ext = jnp.concatenate([ffn_prev.astype(u.dtype), u], axis=1)
    val, gate = jnp.split(causal_dwconv(ext, ffn_dw_w, ffn_dw_b), 2, axis=-1)
    return (val * jax.nn.silu(gate)) @ w_down, ext[:, -(FFN_CONV - 1):]


def decoder_layer(x, pos, mem_k, mem_v, past_k, past_v, conv_prev, pool_prev, ffn_prev,
                  g_mix, w_in, sb_bias, w_sb_out, cv_dw_w, cv_dw_b, cv_ln_g, cv_ln_b, w_cv_out,
                  w_pool_grp, pool_scale, w_pool_out, w_mix_out,
                  g_xattn, w_xq, w_xo, g_ffn, w_up, ffn_dw_w, ffn_dw_b, w_down):
    mix, k_new, v_new, conv_state, pool_state = parallel_mixers(
        rms_norm(x, g_mix), pos, past_k, past_v, conv_prev, pool_prev,
        w_in, sb_bias, w_sb_out, cv_dw_w, cv_dw_b, cv_ln_g, cv_ln_b, w_cv_out,
        w_pool_grp, pool_scale, w_pool_out, w_mix_out)
    x = x + mix
    x = x + cross_attend(rms_norm(x, g_xattn), mem_k, mem_v, w_xq, w_xo)
    ffn, ffn_state = conv_ffn(rms_norm(x, g_ffn), ffn_prev, w_up, ffn_dw_w, ffn_dw_b, w_down)
    x = x + ffn
    return x, k_new, v_new, conv_state, pool_state, ffn_state


def setup_inputs(seed: int = 0) -> dict:
    key = jax.random.key(seed)
    keys = iter(jax.random.split(key, 48))
    f32 = jnp.float32

    def normal(shape, scale=1.0):
        return jax.random.normal(next(keys), shape, f32) * scale

    def gain(shape):
        return 1.0 + 0.05 * normal(shape)

    n_pages = PAST_LEN // PAGE_SIZE
    n_used = DEC_BATCH * n_pages
    n_pool = n_used + max(1, n_used // 4)
    page_table = jax.random.permutation(next(keys), n_pool)[:n_used].reshape(DEC_BATCH, n_pages).astype(jnp.int32)
    L = DEPTH
    return {
        'x_prompt': normal((BATCH, SEQ, D_MODEL)),
        'x_sample': normal((DEC_BATCH, DEC_SEQ, D_MODEL)),
        'mem_prompt': normal((BATCH, N_MEM, D_MODEL)),
        'cache_sb_k': normal((L, n_pool, PAGE_SIZE, SB_HEADS, SB_HEAD_DIM)),
        'cache_sb_v': normal((L, n_pool, PAGE_SIZE, SB_HEADS, SB_HEAD_DIM)),
        'cache_mem_k': normal((L, DEC_BATCH, N_MEM, XA_HEADS, XA_HEAD_DIM)),
        'cache_mem_v': normal((L, DEC_BATCH, N_MEM, XA_HEADS, XA_HEAD_DIM)),
        'state_conv': normal((L, DEC_BATCH, CV_WIDTH - 1, CV_CH), 0.5),
        'state_pool': normal((L, DEC_BATCH, POOL_BUF, POOL_CH)),
        'state_ffn': normal((L, DEC_BATCH, FFN_CONV - 1, 2 * D_FF)),
        'page_table': page_table,
        'g_mix': gain((L, D_MODEL)),
        'w_in': normal((L, D_MODEL, N_IN), D_MODEL ** -0.5),
        'sb_bias': SB_BIAS_INIT + 0.1 * normal((L, SB_HEADS)),
        'w_sb_out': normal((L, SB_WIDTH, D_MODEL), SB_WIDTH ** -0.5),
        'cv_dw_w': normal((L, CV_WIDTH, CV_CH), CV_WIDTH ** -0.5),
        'cv_dw_b': normal((L, CV_CH), 0.02),
        'cv_ln_g': gain((L, CV_CH)),
        'cv_ln_b': normal((L, CV_CH), 0.02),
        'w_cv_out': normal((L, CV_CH, D_MODEL), CV_CH ** -0.5),
        'w_pool_grp': normal((L, POOL_GROUPS, POOL_GC, POOL_GC), POOL_GC ** -0.5),
        'pool_scale': gain((L, POOL_CH)),
        'w_pool_out': normal((L, POOL_CH, D_MODEL), POOL_CH ** -0.5),
        'w_mix_out': normal((L, D_MODEL, D_MODEL), D_MODEL ** -0.5),
        'g_xattn': gain((L, D_MODEL)),
        'g_mem': gain((L, D_MODEL)),
        'w_xq': normal((L, D_MODEL, XA_WIDTH), D_MODEL ** -0.5),
        'w_xkv': normal((L, D_MODEL, 2 * XA_WIDTH), D_MODEL ** -0.5),
        'w_xo': normal((L, XA_WIDTH, D_MODEL), XA_WIDTH ** -0.5),
        'g_ffn': gain((L, D_MODEL)),
        'w_up': normal((L, D_MODEL, 2 * D_FF), D_MODEL ** -0.5),
        'ffn_dw_w': normal((L, FFN_CONV, 2 * D_FF), FFN_CONV ** -0.5),
        'ffn_dw_b': normal((L, 2 * D_FF), 0.02),
        'w_down': normal((L, D_FF, D_MODEL), D_FF ** -0.5),
        'g_final': gain((D_MODEL,)),
    }


def reference(x_prompt, x_sample, mem_prompt, cache_sb_k, cache_sb_v, cache_mem_k, cache_mem_v,
              state_conv, state_pool, state_ffn, page_table,
              g_mix, w_in, sb_bias, w_sb_out, cv_dw_w, cv_dw_b, cv_ln_g, cv_ln_b, w_cv_out,
              w_pool_grp, pool_scale, w_pool_out, w_mix_out,
              g_xattn, g_mem, w_xq, w_xkv, w_xo,
              g_ffn, w_up, ffn_dw_w, ffn_dw_b, w_down, g_final):
    bp, tp, _ = x_prompt.shape
    bs, ts, _ = x_sample.shape
    n_past = page_table.shape[1] * cache_sb_k.shape[2]
    pos_p = jnp.arange(tp, dtype=jnp.int32)
    pos_s = n_past + jnp.arange(ts, dtype=jnp.int32)
    dt = x_prompt.dtype
    conv0 = jnp.zeros((bp, CV_WIDTH - 1, CV_CH), dt)
    pool0 = jnp.zeros((bp, POOL_BUF, POOL_CH), dt)
    ffn0 = jnp.zeros((bp, FFN_CONV - 1, 2 * D_FF), dt)
    xp, xs = x_prompt, x_sample
    kp_l, vp_l, mkp_l, mvp_l, cp_l, pp_l, fp_l = [], [], [], [], [], [], []
    ks_l, vs_l, cs_l, ps_l, fs_l = [], [], [], [], []
    for l in range(DEPTH):
        lw = (g_mix[l], w_in[l], sb_bias[l], w_sb_out[l], cv_dw_w[l], cv_dw_b[l], cv_ln_g[l], cv_ln_b[l], w_cv_out[l],
              w_pool_grp[l], pool_scale[l], w_pool_out[l], w_mix_out[l],
              g_xattn[l], w_xq[l], w_xo[l], g_ffn[l], w_up[l], ffn_dw_w[l], ffn_dw_b[l], w_down[l])
        mk, mv = memory_kv(mem_prompt, g_mem[l], w_xkv[l])
        xp, kp, vp, cp, pp, fp = decoder_layer(xp, pos_p, mk, mv, None, None, conv0, pool0, ffn0, *lw)
        past_k = cache_sb_k[l][page_table].reshape(bs, n_past, SB_HEADS, SB_HEAD_DIM)
        past_v = cache_sb_v[l][page_table].reshape(bs, n_past, SB_HEADS, SB_HEAD_DIM)
        xs, ks, vs, cs, ps, fs = decoder_layer(xs, pos_s, cache_mem_k[l], cache_mem_v[l], past_k, past_v,
                                               state_conv[l], state_pool[l], state_ffn[l], *lw)
        kp_l.append(kp); vp_l.append(vp); mkp_l.append(mk); mvp_l.append(mv)
        cp_l.append(cp); pp_l.append(pp); fp_l.append(fp)
        ks_l.append(ks); vs_l.append(vs); cs_l.append(cs); ps_l.append(ps); fs_l.append(fs)
    y_prompt = rms_norm(xp, g_final)
    y_sample = rms_norm(xs, g_final)
    return (y_prompt, y_sample,
            jnp.stack(kp_l), jnp.stack(vp_l), jnp.stack(mkp_l), jnp.stack(mvp_l),
            jnp.stack(cp_l), jnp.stack(pp_l), jnp.stack(fp_l),
            jnp.stack(ks_l), jnp.stack(vs_l), jnp.stack(cs_l), jnp.stack(ps_l), jnp.stack(fs_l))
```

```python
import functools

import jax
import jax.numpy as jnp
from jax import lax
from jax.experimental import pallas as pl
from jax.experimental.pallas import tpu as pltpu

F32 = jnp.float32
BF16 = jnp.bfloat16

EPS = 1e-6
CV_WIDTH = 31
POOL_WINDOWS = (2, 4, 8, 16)
POOL_BUF = max(POOL_WINDOWS) - 1
FFN_CONV = 3

VMEM_LIMIT_BYTES = 60 * 2**20
LANES = 128
SUBLANES = 8
MXU_DIM = 256

_NT_DIMS = (((1,), (1,)), ((), ()))


def _params(*semantics):
    return pltpu.CompilerParams(dimension_semantics=semantics, vmem_limit_bytes=VMEM_LIMIT_BYTES)


def _dot(a, b):
    return jnp.dot(a, b, preferred_element_type=F32)


def _dot_nt(a, b):
    return lax.dot_general(a, b, _NT_DIMS, preferred_element_type=F32)


def _rmsnorm_kernel(x_ref, g_ref, o_ref):
    x = x_ref[...]
    ms = jnp.mean(x * x, axis=-1, keepdims=True)
    o_ref[...] = (x * lax.rsqrt(ms + EPS) * g_ref[...]).astype(o_ref.dtype)


def rmsnorm(x, g, out_dtype):
    m, d = x.shape
    tr = min(m, 256)
    return pl.pallas_call(
        _rmsnorm_kernel,
        grid=(m // tr,),
        in_specs=[pl.BlockSpec((tr, d), lambda i: (i, 0)),
                  pl.BlockSpec((1, d), lambda i: (0, 0))],
        out_specs=pl.BlockSpec((tr, d), lambda i: (i, 0)),
        out_shape=jax.ShapeDtypeStruct((m, d), out_dtype),
        compiler_params=_params("parallel"),
        name="rmsnorm",
    )(x, g.reshape(1, d))


def _linear_kernel(x_ref, w_ref, *o_refs):
    acc = _dot(x_ref[...].astype(BF16), w_ref[...])
    for o_ref in o_refs:
        o_ref[...] = acc.astype(o_ref.dtype)


def linear(x, w, out_dtypes, tm=1024, tn=1024):
    m, k = x.shape
    n = w.shape[1]
    tm, tn = min(tm, m), min(tn, n)
    outs = pl.pallas_call(
        _linear_kernel,
        grid=(n // tn, m // tm),
        in_specs=[pl.BlockSpec((tm, k), lambda j, i: (i, 0)),
                  pl.BlockSpec((k, tn), lambda j, i: (0, j))],
        out_specs=[pl.BlockSpec((tm, tn), lambda j, i: (i, j)) for _ in out_dtypes],
        out_shape=[jax.ShapeDtypeStruct((m, n), dt) for dt in out_dtypes],
        compiler_params=_params("parallel", "parallel"),
        name="linear",
    )(x, w)
    return outs


def _linear_glu_kernel(x_ref, wa_ref, wb_ref, o_ref):
    x = x_ref[...].astype(BF16)
    o_ref[...] = _dot(x, wa_ref[...]) * jax.nn.sigmoid(_dot(x, wb_ref[...]))


def linear_glu(x, wa, wb, tm=1024, tn=512):
    m, k = x.shape
    n = wa.shape[1]
    tm, tn = min(tm, m), min(tn, n)
    return pl.pallas_call(
        _linear_glu_kernel,
        grid=(n // tn, m // tm),
        in_specs=[pl.BlockSpec((tm, k), lambda j, i: (i, 0)),
                  pl.BlockSpec((k, tn), lambda j, i: (0, j)),
                  pl.BlockSpec((k, tn), lambda j, i: (0, j))],
        out_specs=pl.BlockSpec((tm, tn), lambda j, i: (i, j)),
        out_shape=jax.ShapeDtypeStruct((m, n), F32),
        compiler_params=_params("parallel", "parallel"),
        name="linear_glu",
    )(x, wa, wb)


def _linear_res_kernel(x_ref, w_ref, r_ref, o_ref):
    o_ref[...] = r_ref[...] + _dot(x_ref[...].astype(BF16), w_ref[...])


def linear_res(x, w, res, tm=512, tn=512):
    m, k = x.shape
    n = w.shape[1]
    tm, tn = min(tm, m), min(tn, n)
    return pl.pallas_call(
        _linear_res_kernel,
        grid=(n // tn, m // tm),
        in_specs=[pl.BlockSpec((tm, k), lambda j, i: (i, 0)),
                  pl.BlockSpec((k, tn), lambda j, i: (0, j)),
                  pl.BlockSpec((tm, tn), lambda j, i: (i, j))],
        out_specs=pl.BlockSpec((tm, tn), lambda j, i: (i, j)),
        out_shape=jax.ShapeDtypeStruct((m, n), F32),
        compiler_params=_params("parallel", "parallel"),
        name="linear_res",
    )(x, w, res)


def _merge_kernel(h_ref, a_ref, c_ref, p_ref, wg0_ref, wg1_ref, wg2_ref,
                  wsb_ref, wcv_ref, wpl_ref, o_ref):
    h = h_ref[...].astype(BF16)

    def gated(wg_ref, y_ref, w_ref):
        gate = jax.nn.sigmoid(_dot(h, wg_ref[...]))
        return gate * _dot(y_ref[...].astype(BF16), w_ref[...])

    merged = (gated(wg0_ref, a_ref, wsb_ref) + gated(wg1_ref, c_ref, wcv_ref)
              + gated(wg2_ref, p_ref, wpl_ref))
    o_ref[...] = merged.astype(o_ref.dtype)


def gated_merge(h, attn, conv, pool, w_gates, w_sb_out, w_cv_out, w_pool_out, tm=512, tn=256):
    m, d = h.shape
    n = w_sb_out.shape[1]
    tm = min(tm, m)
    nb = n // tn

    def rows(a):
        return pl.BlockSpec((tm, a.shape[1]), lambda i, j: (i, 0))

    def cols(a, off):
        return pl.BlockSpec((a.shape[0], tn), lambda i, j: (0, j + off))

    return pl.pallas_call(
        _merge_kernel,
        grid=(m // tm, nb),
        in_specs=[rows(h), rows(attn), rows(conv), rows(pool),
                  cols(w_gates, 0), cols(w_gates, nb), cols(w_gates, 2 * nb),
                  cols(w_sb_out, 0), cols(w_cv_out, 0), cols(w_pool_out, 0)],
        out_specs=pl.BlockSpec((tm, tn), lambda i, j: (i, j)),
        out_shape=jax.ShapeDtypeStruct((m, n), BF16),
        compiler_params=_params("parallel", "parallel"),
        name="gated_merge",
    )(h, attn, conv, pool, w_gates, w_gates, w_gates, w_sb_out, w_cv_out, w_pool_out)


def _log_fail(z):
    return -(jnp.maximum(z, 0.0) + jnp.log1p(jnp.exp(-jnp.abs(z))))


def _suffix_sum(f, u):
    f_hi = f.astype(BF16)
    f_lo = (f - f_hi.astype(F32)).astype(BF16)
    return _dot(f_hi, u) + _dot(f_lo, u)


def _sb_block(z, visible, u, r):
    f = _log_fail(z)
    log_hit = z + f
    if visible is not None:
        f = jnp.where(visible, f, 0.0)
    w = jnp.exp(log_hit + _suffix_sum(f, u) + r)
    if visible is not None:
        w = jnp.where(visible, w, 0.0)
    return w, jnp.sum(f, axis=-1, keepdims=True)


def _sb_prompt_kernel(q_ref, k_ref, v_ref, bias_ref, u_ref, o_ref, r_ref, acc_ref, *, tq, scale):
    i = pl.program_id(2)
    q = q_ref[...]
    bias = bias_ref[...]
    u = u_ref[...]
    r_ref[...] = jnp.zeros_like(r_ref)
    acc_ref[...] = jnp.zeros_like(acc_ref)

    def block(kb, visible):
        start = pl.multiple_of(kb * tq, tq)
        k = k_ref[pl.ds(start, tq), :]
        v = v_ref[pl.ds(start, tq), :]
        z = _dot_nt(q, k) * scale + bias
        w, fsum = _sb_block(z, visible, u, r_ref[...])
        acc_ref[...] += _dot(w.astype(BF16), v)
        r_ref[...] += fsum

    row = lax.broadcasted_iota(jnp.int32, (tq, tq), 0)
    col = lax.broadcasted_iota(jnp.int32, (tq, tq), 1)
    block(i, col < row)

    def older(n, carry):
        block(i - 1 - n, None)
        return carry

    lax.fori_loop(0, i, older, 0)
    o_ref[...] = acc_ref[...].astype(o_ref.dtype)


def _suffix_matrix(n):
    j = lax.broadcasted_iota(jnp.int32, (n, n), 0)
    s = lax.broadcasted_iota(jnp.int32, (n, n), 1)
    return (j > s).astype(BF16)


def sb_attention_prompt(q, k, v, bias, batch, seq, heads, tq=256):
    m, width = q.shape
    dh = width // heads
    nq = seq // tq
    bias_b = jnp.broadcast_to(bias.astype(F32)[:, None, None], (heads, 1, tq))
    kv_spec = pl.BlockSpec((seq, dh), lambda b, h, i: (b, h))
    q_spec = pl.BlockSpec((tq, dh), lambda b, h, i: (b * nq + i, h))
    return pl.pallas_call(
        functools.partial(_sb_prompt_kernel, tq=tq, scale=dh ** -0.5),
        grid=(batch, heads, nq),
        in_specs=[q_spec, kv_spec, kv_spec,
                  pl.BlockSpec((None, 1, tq), lambda b, h, i: (h, 0, 0)),
                  pl.BlockSpec((tq, tq), lambda b, h, i: (0, 0))],
        out_specs=q_spec,
        out_shape=jax.ShapeDtypeStruct((m, width), BF16),
        scratch_shapes=[pltpu.VMEM((tq, 1), F32), pltpu.VMEM((tq, dh), F32)],
        compiler_params=_params("parallel", "parallel", "parallel"),
        name="sb_attention_prompt",
    )(q, k, v, bias_b, _suffix_matrix(tq))


def _sb_sample_kernel(pt_ref, q_ref, kn_ref, vn_ref, bias_ref, u_ref, *refs,
                      pages_per_step, n_q, dh, scale):
    del pt_ref
    k_refs = refs[:pages_per_step]
    v_refs = refs[pages_per_step:2 * pages_per_step]
    o_ref, r_ref, acc_ref, qbd_ref = refs[2 * pages_per_step:]
    step = pl.program_id(1)
    page, width = k_refs[0].shape
    cw = 2 * dh
    cr = 2 * n_q
    n_chunk = width // cw
    lane = lax.broadcasted_iota(jnp.int32, (n_q, cw), 1)

    def process(k_page, v_page, visible):
        kb = k_page.astype(BF16)
        vb = v_page.astype(BF16)
        z = jnp.concatenate(
            [_dot_nt(qbd_ref[c * cr:(c + 1) * cr, :], kb[:, c * cw:(c + 1) * cw])
             for c in range(n_chunk)], axis=0) * scale + bias_ref[...]
        w, fsum = _sb_block(z, visible, u_ref[...], r_ref[...])
        wb = w.astype(BF16)
        for c in range(n_chunk):
            acc_ref[c * cr:(c + 1) * cr, :] += _dot(wb[c * cr:(c + 1) * cr, :],
                                                    vb[:, c * cw:(c + 1) * cw])
        r_ref[...] += fsum

    @pl.when(step == 0)
    def _():
        r_ref[...] = jnp.zeros_like(r_ref)
        acc_ref[...] = jnp.zeros_like(acc_ref)
        q = q_ref[...]
        for c in range(n_chunk):
            qc = q[:, c * cw:(c + 1) * cw]
            qbd_ref[c * cr:(c + 1) * cr, :] = jnp.concatenate(
                [jnp.where(lane < dh, qc, 0.0), jnp.where(lane >= dh, qc, 0.0)], axis=0).astype(BF16)
        pad = jnp.zeros((page - n_q, width), F32)
        row = lax.broadcasted_iota(jnp.int32, (n_chunk * cr, page), 0)
        col = lax.broadcasted_iota(jnp.int32, (n_chunk * cr, page), 1)
        process(jnp.concatenate([kn_ref[...], pad], axis=0),
                jnp.concatenate([vn_ref[...], pad], axis=0), col < row % n_q)

    for k_ref, v_ref in zip(k_refs, v_refs):
        process(k_ref[...], v_ref[...], None)

    @pl.when(step == pl.num_programs(1) - 1)
    def _():
        for c in range(n_chunk):
            o_ref[:, c * cw:(c + 1) * cw] = jnp.where(
                lane < dh, acc_ref[c * cr:c * cr + n_q, :], acc_ref[c * cr + n_q:(c + 1) * cr, :])


def sb_attention_sample(q, k_new, v_new, bias, cache_k, cache_v, page_table, layer, heads,
                        pages_per_step=4):
    batch, n_q, width = q.shape
    dh = width // heads
    page = cache_k.shape[2]
    n_pages = page_table.shape[1]
    assert n_q == SUBLANES and page == LANES and n_pages % pages_per_step == 0
    rows = heads * n_q
    bias_rows = jnp.repeat(bias.astype(F32), n_q).reshape(rows, 1)

    def new_spec():
        return pl.BlockSpec((None, n_q, width), lambda b, s, pt: (b, 0, 0))

    def page_spec(slot):
        def index_map(b, s, pt):
            return (layer, pt[b, n_pages - 1 - (s * pages_per_step + slot)], 0, 0)
        return pl.BlockSpec((None, None, page, width), index_map)

    page_specs = [page_spec(i) for i in range(pages_per_step)]
    grid_spec = pltpu.PrefetchScalarGridSpec(
        num_scalar_prefetch=1,
        grid=(batch, n_pages // pages_per_step),
        in_specs=[new_spec(), new_spec(), new_spec(),
                  pl.BlockSpec((rows, 1), lambda b, s, pt: (0, 0)),
                  pl.BlockSpec((page, page), lambda b, s, pt: (0, 0))] + page_specs + page_specs,
        out_specs=new_spec(),
        scratch_shapes=[pltpu.VMEM((rows, 1), F32), pltpu.VMEM((rows, 2 * dh), F32),
                        pltpu.VMEM((rows, 2 * dh), BF16)],
    )
    return pl.pallas_call(
        functools.partial(_sb_sample_kernel, pages_per_step=pages_per_step, n_q=n_q, dh=dh,
                          scale=dh ** -0.5),
        grid_spec=grid_spec,
        out_shape=jax.ShapeDtypeStruct((batch, n_q, width), F32),
        compiler_params=_params("parallel", "arbitrary"),
        name="sb_attention_sample",
    )(page_table, q, k_new, v_new, bias_rows, _suffix_matrix(page),
      *([cache_k] * pages_per_step), *([cache_v] * pages_per_step))


def _conv_branch_kernel(x_ref, prev_ref, w_ref, b_ref, g_ref, beta_ref, o_ref, ext_ref, y_ref,
                        *, tt, rc, cc):
    t = pl.program_id(1)
    halo = CV_WIDTH - 1
    top = ext_ref.shape[0] - tt
    ch = x_ref.shape[1]

    @pl.when(t == 0)
    def _():
        ext_ref[top - halo:top, :] = prev_ref[...]

    @pl.when(t > 0)
    def _():
        ext_ref[0:top, :] = ext_ref[tt:tt + top, :]

    ext_ref[top:top + tt, :] = x_ref[...]

    for c0 in range(0, ch, cc):
        cs = slice(c0, c0 + cc)
        for r0 in range(0, tt, rc):
            acc = jnp.broadcast_to(b_ref[:, cs], (rc, cc))
            for j in range(CV_WIDTH):
                lo = top - halo + r0 + j
                acc = acc + ext_ref[lo:lo + rc, cs] * w_ref[j:j + 1, cs]
            y_ref[r0:r0 + rc, cs] = acc

    nr = min(tt, 32)
    for r0 in range(0, tt, nr):
        y = y_ref[r0:r0 + nr, :]
        yc = y - jnp.mean(y, axis=-1, keepdims=True)
        var = jnp.mean(yc * yc, axis=-1, keepdims=True)
        z = yc * lax.rsqrt(var + EPS) * g_ref[...] + beta_ref[...]
        o_ref[r0:r0 + nr, :] = (z * jax.nn.sigmoid(z)).astype(o_ref.dtype)


def conv_branch(glu, prev, dw_w, dw_b, ln_g, ln_b, out_dtype, tt=256):
    b, t, ch = glu.shape
    tt = min(tt, t)
    rc = min(tt, 64)
    top = 32
    seq_spec = pl.BlockSpec((None, tt, ch), lambda i, j: (i, j, 0))
    vec_spec = pl.BlockSpec((1, ch), lambda i, j: (0, 0))
    return pl.pallas_call(
        functools.partial(_conv_branch_kernel, tt=tt, rc=rc, cc=256),
        grid=(b, t // tt),
        in_specs=[seq_spec,
                  pl.BlockSpec((None, CV_WIDTH - 1, ch), lambda i, j: (i, 0, 0)),
                  pl.BlockSpec((CV_WIDTH, ch), lambda i, j: (0, 0)),
                  vec_spec, vec_spec, vec_spec],
        out_specs=seq_spec,
        out_shape=jax.ShapeDtypeStruct((b, t, ch), out_dtype),
        scratch_shapes=[pltpu.VMEM((top + tt, ch), F32), pltpu.VMEM((tt, ch), F32)],
        compiler_params=_params("parallel", "arbitrary"),
        name="conv_branch",
    )(glu, prev, dw_w, dw_b.reshape(1, ch), ln_g.reshape(1, ch), ln_b.reshape(1, ch))


def _pool_branch_kernel(x_ref, prev_ref, wg_ref, sc_ref, o_ref, ext_ref, pm_ref, *, tt, rc, pos0):
    t = pl.program_id(1)
    top = ext_ref.shape[0] - tt
    gc = wg_ref.shape[1]

    @pl.when(t == 0)
    def _():
        ext_ref[top - POOL_BUF:top, :] = prev_ref[...]

    @pl.when(t > 0)
    def _():
        ext_ref[0:top, :] = ext_ref[tt:tt + top, :]

    ext_ref[top:top + tt, :] = x_ref[...]

    for g, win in enumerate(POOL_WINDOWS):
        cs = slice(g * gc, (g + 1) * gc)
        for r0 in range(0, tt, rc):
            cur = ext_ref[top + r0:top + r0 + rc, cs]
            acc = cur
            for i in range(1, win):
                acc = acc + ext_ref[top + r0 - i:top + r0 - i + rc, cs]
            pos = pos0 + t * tt + r0 + lax.broadcasted_iota(jnp.int32, (rc, gc), 0)
            count = jnp.minimum(pos + 1, win).astype(F32)
            pm_ref[r0:r0 + rc, cs] = acc / count - cur
        y = _dot(pm_ref[:, cs].astype(BF16), wg_ref[g]) * sc_ref[:, cs]
        o_ref[:, cs] = y.astype(o_ref.dtype)


def pool_branch(x, prev, w_grp, scale, pos0, out_dtype, tt=256):
    b, t, ch = x.shape
    tt = min(tt, t)
    rc = min(tt, 64)
    top = 16
    seq_spec = pl.BlockSpec((None, tt, ch), lambda i, j: (i, j, 0))
    return pl.pallas_call(
        functools.partial(_pool_branch_kernel, tt=tt, rc=rc, pos0=pos0),
        grid=(b, t // tt),
        in_specs=[seq_spec,
                  pl.BlockSpec((None, POOL_BUF, ch), lambda i, j: (i, 0, 0)),
                  pl.BlockSpec(w_grp.shape, lambda i, j: (0, 0, 0)),
                  pl.BlockSpec((1, ch), lambda i, j: (0, 0))],
        out_specs=seq_spec,
        out_shape=jax.ShapeDtypeStruct((b, t, ch), out_dtype),
        scratch_shapes=[pltpu.VMEM((top + tt, ch), F32), pltpu.VMEM((tt, ch), F32)],
        compiler_params=_params("parallel", "arbitrary"),
        name="pool_branch",
    )(x, prev, w_grp, scale.reshape(1, ch))


def _cross_attn_kernel(q_ref, mk_ref, mv_ref, o_ref, *, heads, scale):
    dh = q_ref.shape[1] // heads
    for h in range(heads):
        cs = slice(h * dh, (h + 1) * dh)
        s = _dot_nt(q_ref[:, cs].astype(BF16), mk_ref[:, cs].astype(BF16)) * scale
        e = jnp.exp(s - jnp.max(s, axis=-1, keepdims=True))
        a = e / jnp.sum(e, axis=-1, keepdims=True)
        o_ref[:, cs] = _dot(a.astype(BF16), mv_ref[:, cs].astype(BF16)).astype(o_ref.dtype)


def cross_attention(q, mem_k, mem_v, heads, out_dtype, tq=512):
    b, t, width = q.shape
    n_mem = mem_k.shape[1]
    tq = min(tq, t)
    q_spec = pl.BlockSpec((None, tq, width), lambda i, j: (i, j, 0))
    mem_spec = pl.BlockSpec((None, n_mem, width), lambda i, j: (i, 0, 0))
    return pl.pallas_call(
        functools.partial(_cross_attn_kernel, heads=heads, scale=(width // heads) ** -0.5),
        grid=(b, t // tq),
        in_specs=[q_spec, mem_spec, mem_spec],
        out_specs=q_spec,
        out_shape=jax.ShapeDtypeStruct((b, t, width), out_dtype),
        compiler_params=_params("parallel", "parallel"),
        name="cross_attention",
    )(q, mem_k, mem_v)


def _ffn_up_kernel(x_ref, wv_ref, wg_ref, cwv_ref, cwg_ref, cbv_ref, cbg_ref, *refs,
                   tm, tiles_per_seq, seq_len):
    pad = SUBLANES
    if tiles_per_seq:
        act_ref, tailv_ref, tailg_ref, extv_ref, extg_ref = refs
        fixes = (None, None)
    else:
        f1v_ref, f2v_ref, f1g_ref, f2g_ref, act_ref, tailv_ref, tailg_ref, extv_ref, extg_ref = refs
        fixes = ((f1v_ref, f2v_ref), (f1g_ref, f2g_ref))
    i = pl.program_id(1)
    x = x_ref[...].astype(BF16)

    def conv(w_ref, cw_ref, cb_ref, tail_ref, ext_ref, fix):
        u = _dot(x, w_ref[...])
        if tiles_per_seq:
            first = i % tiles_per_seq == 0

            @pl.when(first)
            def _():
                ext_ref[0:pad, :] = jnp.zeros((pad, ext_ref.shape[1]), F32)

            @pl.when(jnp.logical_not(first))
            def _():
                ext_ref[0:pad, :] = ext_ref[tm:tm + pad, :]

            tail_ref[...] = u[tm - pad:, :]
        else:
            ext_ref[0:pad, :] = jnp.zeros((pad, ext_ref.shape[1]), F32)
            tail_ref[...] = u
        ext_ref[pad:pad + tm, :] = u
        u1 = ext_ref[pad - 1:pad - 1 + tm, :]
        u2 = ext_ref[pad - 2:pad - 2 + tm, :]
        if fix is not None:
            t_in_seq = lax.broadcasted_iota(jnp.int32, u.shape, 0) % seq_len
            u1 = jnp.where(t_in_seq < 1, fix[0][...], u1)
            u2 = jnp.where(t_in_seq < 2, fix[1][...], u2)
        return cb_ref[...] + cw_ref[0:1, :] * u2 + cw_ref[1:2, :] * u1 + cw_ref[2:3, :] * u

    val = conv(wv_ref, cwv_ref, cbv_ref, tailv_ref, extv_ref, fixes[0])
    gate = conv(wg_ref, cwg_ref, cbg_ref, tailg_ref, extg_ref, fixes[1])
    act_ref[...] = (val * (gate * jax.nn.sigmoid(gate))).astype(act_ref.dtype)


def ffn_up(x, w_up, dw_w, dw_b, seq_len, prev, tm=1024, tn=512):
    m, d = x.shape
    f = w_up.shape[1] // 2
    nb = f // tn
    tm = min(tm, m)
    n_tiles = m // tm
    x_spec = pl.BlockSpec((tm, d), lambda j, i: (i, 0))
    col = lambda rows, off: pl.BlockSpec((rows, tn), lambda j, i: (0, j + off))
    tile = lambda off: pl.BlockSpec((tm, tn), lambda j, i: (i, j + off))
    in_specs = [x_spec, col(d, 0), col(d, nb), col(FFN_CONV, 0), col(FFN_CONV, nb), col(1, 0), col(1, nb)]
    args = [x, w_up, w_up, dw_w, dw_w, dw_b.reshape(1, 2 * f), dw_b.reshape(1, 2 * f)]
    if prev is None:
        assert seq_len % tm == 0
        tiles_per_seq = seq_len // tm
        tail_rows = SUBLANES
    else:
        assert n_tiles == 1 and seq_len >= FFN_CONV - 1
        tiles_per_seq = 0
        tail_rows = tm
        batch = m // seq_len
        zeros = jnp.zeros((batch, seq_len, 2 * f), F32)
        fix1 = zeros.at[:, 0].set(prev[:, 1]).reshape(m, 2 * f)
        fix2 = zeros.at[:, 0].set(prev[:, 0]).at[:, 1].set(prev[:, 1]).reshape(m, 2 * f)
        in_specs += [tile(0), tile(0), tile(nb), tile(nb)]
        args += [fix1, fix2, fix1, fix2]
    tail_spec = pl.BlockSpec((None, tail_rows, tn), lambda j, i: (i, 0, j))
    tail_shape = jax.ShapeDtypeStruct((n_tiles, tail_rows, f), F32)
    return pl.pallas_call(
        functools.partial(_ffn_up_kernel, tm=tm, tiles_per_seq=tiles_per_seq, seq_len=seq_len),
        grid=(nb, n_tiles),
        in_specs=in_specs,
        out_specs=[pl.BlockSpec((tm, tn), lambda j, i: (i, j)), tail_spec, tail_spec],
        out_shape=[jax.ShapeDtypeStruct((m, f), BF16), tail_shape, tail_shape],
        scratch_shapes=[pltpu.VMEM((SUBLANES + tm, tn), F32), pltpu.VMEM((SUBLANES + tm, tn), F32)],
        compiler_params=_params("parallel", "arbitrary"),
        name="ffn_up",
    )(*args)


def _pad_halves(a, half, half_pad):
    widths = [(0, 0)] * (a.ndim - 1) + [(0, half_pad - half)]
    return jnp.concatenate([jnp.pad(a[..., :half], widths), jnp.pad(a[..., half:], widths)], axis=-1)


def _layer_weights(l, p, d_ff_pad):
    d = p['w_in'].shape[1]
    sbw = p['w_sb_out'].shape[1]
    cvc = p['w_cv_out'].shape[1]
    plc = p['w_pool_out'].shape[1]
    w_in = p['w_in'][l]
    offs = [0, sbw, 2 * sbw, 3 * sbw, 3 * sbw + cvc, 3 * sbw + 2 * cvc, 3 * sbw + 2 * cvc + plc]
    names = ['w_q', 'w_k', 'w_v', 'w_cva', 'w_cvb', 'w_pool_in']
    w = {n: w_in[:, a:b].astype(BF16) for n, a, b in zip(names, offs[:-1], offs[1:])}
    w['w_gates'] = w_in[:, offs[-1]:].astype(BF16)
    for n in ('w_sb_out', 'w_cv_out', 'w_pool_out', 'w_mix_out', 'w_xq', 'w_xo', 'w_pool_grp'):
        w[n] = p[n][l].astype(BF16)
    xw = p['w_xkv'].shape[2] // 2
    w['w_xk'] = p['w_xkv'][l][:, :xw].astype(BF16)
    w['w_xv'] = p['w_xkv'][l][:, xw:].astype(BF16)
    d_ff = p['w_down'].shape[1]
    w['w_up'] = _pad_halves(p['w_up'][l], d_ff, d_ff_pad).astype(BF16)
    w['ffn_dw_w'] = _pad_halves(p['ffn_dw_w'][l], d_ff, d_ff_pad)
    w['ffn_dw_b'] = _pad_halves(p['ffn_dw_b'][l], d_ff, d_ff_pad)
    w['w_down'] = jnp.pad(p['w_down'][l], ((0, d_ff_pad - d_ff), (0, 0))).astype(BF16)
    for n in ('g_mix', 'sb_bias', 'cv_dw_w', 'cv_dw_b', 'cv_ln_g', 'cv_ln_b', 'pool_scale',
              'g_xattn', 'g_ffn'):
        w[n] = p[n][l]
    return w


def _state(prev, cur, n):
    return jnp.concatenate([prev, cur], axis=1)[:, -n:]


def _decoder_layer(x, batch, seq, w, mem_k, mem_v, xa_heads, conv_prev, pool_prev, ffn_prev, past, d_ff):
    m, d = x.shape
    heads = w['sb_bias'].shape[0]
    prompt = past is None
    act_dt = BF16 if prompt else F32
    three = lambda a: a.reshape(batch, seq, a.shape[-1])
    two = lambda a: a.reshape(m, a.shape[-1])

    h = rmsnorm(x, w['g_mix'], BF16)
    glu = linear_glu(h, w['w_cva'], w['w_cvb'])
    pool_in, = linear(h, w['w_pool_in'], (F32,))
    if prompt:
        q, = linear(h, w['w_q'], (BF16,))
        k, kb = linear(h, w['w_k'], (F32, BF16))
        v, vb = linear(h, w['w_v'], (F32, BF16))
        attn = sb_attention_prompt(q, kb, vb, w['sb_bias'], batch, seq, heads)
        pos0 = 0
    else:
        q, = linear(h, w['w_q'], (F32,))
        k, = linear(h, w['w_k'], (F32,))
        v, = linear(h, w['w_v'], (F32,))
        cache_k, cache_v, page_table, layer = past
        attn = two(sb_attention_sample(three(q), three(k), three(v), w['sb_bias'],
                                       cache_k, cache_v, page_table, layer, heads))
        pos0 = page_table.shape[1] * cache_k.shape[2]
    conv = conv_branch(three(glu), conv_prev, w['cv_dw_w'], w['cv_dw_b'], w['cv_ln_g'], w['cv_ln_b'], act_dt)
    pool = pool_branch(three(pool_in), pool_prev, w['w_pool_grp'], w['pool_scale'], pos0, act_dt)
    merged = gated_merge(h, attn, two(conv), two(pool), w['w_gates'],
                         w['w_sb_out'], w['w_cv_out'], w['w_pool_out'])
    x = linear_res(merged, w['w_mix_out'], x)

    h = rmsnorm(x, w['g_xattn'], BF16)
    qx, = linear(h, w['w_xq'], (act_dt,))
    xa = cross_attention(three(qx), mem_k, mem_v, xa_heads, act_dt)
    x = linear_res(two(xa), w['w_xo'], x)

    h = rmsnorm(x, w['g_ffn'], BF16)
    d_ff_pad = w['w_down'].shape[0]
    act, tail_v, tail_g = ffn_up(h, w['w_up'], w['ffn_dw_w'], w['ffn_dw_b'], seq,
                                 None if prompt else _pad_halves(ffn_prev, d_ff, d_ff_pad))
    x = linear_res(act, w['w_down'], x)

    tails = []
    for tail in (tail_v, tail_g):
        if prompt:
            per_seq = tail.shape[0] // batch
            tails.append(tail[per_seq - 1::per_seq, SUBLANES - (FFN_CONV - 1):, :d_ff])
        else:
            tails.append(tail.reshape(batch, seq, -1)[:, seq - (FFN_CONV - 1):, :d_ff])
    ffn_state = jnp.concatenate(tails, axis=-1)
    conv_state = _state(conv_prev, three(glu), CV_WIDTH - 1)
    pool_state = _state(pool_prev, three(pool_in), POOL_BUF)
    return x, k, v, conv_state, pool_state, ffn_state


def kernel(x_prompt, x_sample, mem_prompt, cache_sb_k, cache_sb_v, cache_mem_k, cache_mem_v, state_conv, state_pool, state_ffn, page_table, g_mix, w_in, sb_bias, w_sb_out, cv_dw_w, cv_dw_b, cv_ln_g, cv_ln_b, w_cv_out, w_pool_grp, pool_scale, w_pool_out, w_mix_out, g_xattn, g_mem, w_xq, w_xkv, w_xo, g_ffn, w_up, ffn_dw_w, ffn_dw_b, w_down, g_final):
    p = dict(g_mix=g_mix, w_in=w_in, sb_bias=sb_bias, w_sb_out=w_sb_out, cv_dw_w=cv_dw_w,
             cv_dw_b=cv_dw_b, cv_ln_g=cv_ln_g, cv_ln_b=cv_ln_b, w_cv_out=w_cv_out,
             w_pool_grp=w_pool_grp, pool_scale=pool_scale, w_pool_out=w_pool_out,
             w_mix_out=w_mix_out, g_xattn=g_xattn, w_xq=w_xq, w_xkv=w_xkv, w_xo=w_xo,
             g_ffn=g_ffn, w_up=w_up, ffn_dw_w=ffn_dw_w, ffn_dw_b=ffn_dw_b, w_down=w_down)
    depth = w_in.shape[0]
    bp, tp, d = x_prompt.shape
    bs, ts, _ = x_sample.shape
    heads, dh = cache_sb_k.shape[3], cache_sb_k.shape[4]
    xa_heads, xa_dh = cache_mem_k.shape[3], cache_mem_k.shape[4]
    n_mem = mem_prompt.shape[1]
    d_ff = w_down.shape[1]
    ffn_tn = 512
    d_ff_pad = -(-d_ff // ffn_tn) * ffn_tn
    cache_k = cache_sb_k.reshape(cache_sb_k.shape[:3] + (heads * dh,))
    cache_v = cache_sb_v.reshape(cache_sb_v.shape[:3] + (heads * dh,))
    mem_flat = mem_prompt.reshape(bp * n_mem, d)

    xp = x_prompt.reshape(bp * tp, d)
    xs = x_sample.reshape(bs * ts, d)
    conv0 = jnp.zeros((bp,) + state_conv.shape[2:], F32)
    pool0 = jnp.zeros((bp,) + state_pool.shape[2:], F32)
    outs_p, outs_s = [], []
    for l in range(depth):
        w = _layer_weights(l, p, d_ff_pad)
        mem_h = rmsnorm(mem_flat, g_mem[l], BF16)
        mk, = linear(mem_h, w['w_xk'], (F32,))
        mv, = linear(mem_h, w['w_xv'], (F32,))
        mk3 = mk.reshape(bp, n_mem, xa_heads * xa_dh)
        mv3 = mv.reshape(bp, n_mem, xa_heads * xa_dh)
        xp, kp, vp, cp, pp, fp = _decoder_layer(xp, bp, tp, w, mk3, mv3, xa_heads, conv0, pool0, None,
                                                None, d_ff)
        xs, ks, vs, cs, ps, fs = _decoder_layer(
            xs, bs, ts, w, cache_mem_k[l].reshape(bs, n_mem, -1), cache_mem_v[l].reshape(bs, n_mem, -1),
            xa_heads, state_conv[l], state_pool[l], state_ffn[l], (cache_k, cache_v, page_table, l), d_ff)
        outs_p.append((kp.reshape(bp, tp, heads, dh), vp.reshape(bp, tp, heads, dh),
                       mk.reshape(bp, n_mem, xa_heads, xa_dh), mv.reshape(bp, n_mem, xa_heads, xa_dh),
                       cp, pp, fp))
        outs_s.append((ks.reshape(bs, ts, heads, dh), vs.reshape(bs, ts, heads, dh), cs, ps, fs))
    y_prompt = rmsnorm(xp, g_final, F32).reshape(bp, tp, d)
    y_sample = rmsnorm(xs, g_final, F32).reshape(bs, ts, d)
    stack = lambda rows, i: jnp.stack([r[i] for r in rows])
    return ((y_prompt, y_sample) + tuple(stack(outs_p, i) for i in range(7))
            + tuple(stack(outs_s, i) for i in range(5)))
```

```python
import functools

import jax
import jax.numpy as jnp
from jax import lax
from jax.experimental import pallas as pl
from jax.experimental.pallas import tpu as pltpu

F32 = jnp.float32
BF16 = jnp.bfloat16

EPS = 1e-6
CV_WIDTH = 31
POOL_WINDOWS = (2, 4, 8, 16)
POOL_BUF = max(POOL_WINDOWS) - 1
FFN_CONV = 3

VMEM_LIMIT_BYTES = 60 * 2**20
LANES = 128
SUBLANES = 8
MXU_DIM = 256

_NT_DIMS = (((1,), (1,)), ((), ()))


def _params(*semantics):
    return pltpu.CompilerParams(dimension_semantics=semantics, vmem_limit_bytes=VMEM_LIMIT_BYTES)


def _dot(a, b):
    return jnp.dot(a, b, preferred_element_type=F32)


def _dot_nt(a, b):
    return lax.dot_general(a, b, _NT_DIMS, preferred_element_type=F32)


def _rmsnorm_kernel(x_ref, g_ref, o_ref):
    x = x_ref[...]
    ms = jnp.mean(x * x, axis=-1, keepdims=True)
    o_ref[...] = (x * lax.rsqrt(ms + EPS) * g_ref[...]).astype(o_ref.dtype)


def rmsnorm(x, g, out_dtype):
    m, d = x.shape
    tr = min(m, 256)
    return pl.pallas_call(
        _rmsnorm_kernel,
        grid=(m // tr,),
        in_specs=[pl.BlockSpec((tr, d), lambda i: (i, 0)),
                  pl.BlockSpec((1, d), lambda i: (0, 0))],
        out_specs=pl.BlockSpec((tr, d), lambda i: (i, 0)),
        out_shape=jax.ShapeDtypeStruct((m, d), out_dtype),
        compiler_params=_params("parallel"),
        name="rmsnorm",
    )(x, g.reshape(1, d))


def _linear_kernel(x_ref, w_ref, *o_refs):
    acc = _dot(x_ref[...].astype(BF16), w_ref[...])
    for o_ref in o_refs:
        o_ref[...] = acc.astype(o_ref.dtype)


def linear(x, w, out_dtypes, tm=1024, tn=1024):
    m, k = x.shape
    n = w.shape[1]
    tm, tn = min(tm, m), min(tn, n)
    outs = pl.pallas_call(
        _linear_kernel,
        grid=(n // tn, m // tm),
        in_specs=[pl.BlockSpec((tm, k), lambda j, i: (i, 0)),
                  pl.BlockSpec((k, tn), lambda j, i: (0, j))],
        out_specs=[pl.BlockSpec((tm, tn), lambda j, i: (i, j)) for _ in out_dtypes],
        out_shape=[jax.ShapeDtypeStruct((m, n), dt) for dt in out_dtypes],
        compiler_params=_params("parallel", "parallel"),
        name="linear",
    )(x, w)
    return outs


def _linear_glu_kernel(x_ref, wa_ref, wb_ref, o_ref):
    x = x_ref[...].astype(BF16)
    o_ref[...] = _dot(x, wa_ref[...]) * jax.nn.sigmoid(_dot(x, wb_ref[...]))


def linear_glu(x, wa, wb, tm=1024, tn=512):
    m, k = x.shape
    n = wa.shape[1]
    tm, tn = min(tm, m), min(tn, n)
    return pl.pallas_call(
        _linear_glu_kernel,
        grid=(n // tn, m // tm),
        in_specs=[pl.BlockSpec((tm, k), lambda j, i: (i, 0)),
                  pl.BlockSpec((k, tn), lambda j, i: (0, j)),
                  pl.BlockSpec((k, tn), lambda j, i: (0, j))],
        out_specs=pl.BlockSpec((tm, tn), lambda j, i: (i, j)),
        out_shape=jax.ShapeDtypeStruct((m, n), F32),
        compiler_params=_params("parallel", "parallel"),
        name="linear_glu",
    )(x, wa, wb)


def _linear_res_kernel(x_ref, w_ref, r_ref, o_ref):
    o_ref[...] = r_ref[...] + _dot(x_ref[...].astype(BF16), w_ref[...])


def linear_res(x, w, res, tm=512, tn=512):
    m, k = x.shape
    n = w.shape[1]
    tm, tn = min(tm, m), min(tn, n)
    return pl.pallas_call(
        _linear_res_kernel,
        grid=(n // tn, m // tm),
        in_specs=[pl.BlockSpec((tm, k), lambda j, i: (i, 0)),
                  pl.BlockSpec((k, tn), lambda j, i: (0, j)),
                  pl.BlockSpec((tm, tn), lambda j, i: (i, j))],
        out_specs=pl.BlockSpec((tm, tn), lambda j, i: (i, j)),
        out_shape=jax.ShapeDtypeStruct((m, n), F32),
        compiler_params=_params("parallel", "parallel"),
        name="linear_res",
    )(x, w, res)


def _merge_kernel(h_ref, a_ref, c_ref, p_ref, wg0_ref, wg1_ref, wg2_ref,
                  wsb_ref, wcv_ref, wpl_ref, o_ref):
    h = h_ref[...].astype(BF16)

    def gated(wg_ref, y_ref, w_ref):
        gate = jax.nn.sigmoid(_dot(h, wg_ref[...]))
        return gate * _dot(y_ref[...].astype(BF16), w_ref[...])

    merged = (gated(wg0_ref, a_ref, wsb_ref) + gated(wg1_ref, c_ref, wcv_ref)
              + gated(wg2_ref, p_ref, wpl_ref))
    o_ref[...] = merged.astype(o_ref.dtype)


def gated_merge(h, attn, conv, pool, w_gates, w_sb_out, w_cv_out, w_pool_out, tm=512, tn=256):
    m, d = h.shape
    n = w_sb_out.shape[1]
    tm = min(tm, m)
    nb = n // tn

    def rows(a):
        return pl.BlockSpec((tm, a.shape[1]), lambda i, j: (i, 0))

    def cols(a, off):
        return pl.BlockSpec((a.shape[0], tn), lambda i, j: (0, j + off))

    return pl.pallas_call(
        _merge_kernel,
        grid=(m // tm, nb),
        in_specs=[rows(h), rows(attn), rows(conv), rows(pool),
                  cols(w_gates, 0), cols(w_gates, nb), cols(w_gates, 2 * nb),
                  cols(w_sb_out, 0), cols(w_cv_out, 0), cols(w_pool_out, 0)],
        out_specs=pl.BlockSpec((tm, tn), lambda i, j: (i, j)),
        out_shape=jax.ShapeDtypeStruct((m, n), BF16),
        compiler_params=_params("parallel", "parallel"),
        name="gated_merge",
    )(h, attn, conv, pool, w_gates, w_gates, w_gates, w_sb_out, w_cv_out, w_pool_out)


def _log_fail(z):
    return -(jnp.maximum(z, 0.0) + jnp.log(1.0 + jnp.exp(-jnp.abs(z))))


def _suffix_sum(f, u):
    f_hi = f.astype(BF16)
    f_lo = (f - f_hi.astype(F32)).astype(BF16)
    return _dot(f_hi, u) + _dot(f_lo, u)


def _sb_blocks(zs, visible, u):
    fs = [_log_fail(z) for z in zs]
    log_hits = [z + f for z, f in zip(zs, fs)]
    if visible is not None:
        fs = [jnp.where(visible, f, 0.0) for f in fs]
    sufs = [_suffix_sum(f, u) for f in fs]
    return ([lh + s for lh, s in zip(log_hits, sufs)],
            [jnp.sum(f, axis=-1, keepdims=True) for f in fs])


def _sb_weights(log_w, r, visible):
    w = jnp.exp(log_w + r)
    if visible is not None:
        w = jnp.where(visible, w, 0.0)
    return w.astype(BF16)


def _sb_prompt_kernel(q_ref, k_ref, v_ref, bias_ref, u_ref, o_ref, r_ref, acc_ref, *, tq, dh, scale):
    i = pl.program_id(2)
    u = u_ref[...]
    r_ref[...] = jnp.zeros_like(r_ref)
    acc_ref[...] = jnp.zeros_like(acc_ref)

    def block(kb, visible):
        start = pl.multiple_of(kb * tq, tq)
        cols = [slice(hh * dh, (hh + 1) * dh) for hh in range(q_ref.shape[1] // dh)]
        zs = [_dot_nt(q_ref[:, cs], k_ref[pl.ds(start, tq), cs]) * scale + bias_ref[hh:hh + 1, :]
              for hh, cs in enumerate(cols)]
        log_ws, fsums = _sb_blocks(zs, visible, u)
        for hh, cs in enumerate(cols):
            w = _sb_weights(log_ws[hh], r_ref[hh], visible)
            acc_ref[:, cs] += _dot(w, v_ref[pl.ds(start, tq), cs])
            r_ref[hh] += fsums[hh]

    row = lax.broadcasted_iota(jnp.int32, (tq, tq), 0)
    col = lax.broadcasted_iota(jnp.int32, (tq, tq), 1)
    block(i, col < row)

    def older(n, carry):
        block(i - 1 - n, None)
        return carry

    lax.fori_loop(0, i, older, 0)
    o_ref[...] = acc_ref[...].astype(o_ref.dtype)


def _suffix_matrix(n):
    j = lax.broadcasted_iota(jnp.int32, (n, n), 0)
    s = lax.broadcasted_iota(jnp.int32, (n, n), 1)
    return (j > s).astype(BF16)


def sb_attention_prompt(q, k, v, bias, batch, seq, heads, tq=256, heads_per_step=4):
    m, width = q.shape
    dh = width // heads
    nq = seq // tq
    hps = heads_per_step
    bias_b = jnp.broadcast_to(bias.astype(F32).reshape(heads // hps, hps, 1), (heads // hps, hps, tq))
    kv_spec = pl.BlockSpec((seq, hps * dh), lambda b, h, i: (b, h))
    q_spec = pl.BlockSpec((tq, hps * dh), lambda b, h, i: (b * nq + i, h))
    return pl.pallas_call(
        functools.partial(_sb_prompt_kernel, tq=tq, dh=dh, scale=dh ** -0.5),
        grid=(batch, heads // hps, nq),
        in_specs=[q_spec, kv_spec, kv_spec,
                  pl.BlockSpec((None, hps, tq), lambda b, h, i: (h, 0, 0)),
                  pl.BlockSpec((tq, tq), lambda b, h, i: (0, 0))],
        out_specs=q_spec,
        out_shape=jax.ShapeDtypeStruct((m, width), BF16),
        scratch_shapes=[pltpu.VMEM((hps, tq, 1), F32), pltpu.VMEM((tq, hps * dh), F32)],
        compiler_params=_params("parallel", "parallel", "parallel"),
        name="sb_attention_prompt",
    )(q, k, v, bias_b, _suffix_matrix(tq))


def _sb_sample_kernel(pt_ref, q_ref, kn_ref, vn_ref, bias_ref, u_ref, *refs,
                      pages_per_step, n_q, dh, scale):
    del pt_ref
    k_refs = refs[:pages_per_step]
    v_refs = refs[pages_per_step:2 * pages_per_step]
    o_ref, r_ref, acc_ref, qbd_ref = refs[2 * pages_per_step:]
    step = pl.program_id(1)
    width = q_ref.shape[1]
    heads = width // dh
    page = k_refs[0].shape[0] // heads
    cw = 2 * dh
    cr = 2 * n_q
    n_chunk = width // cw
    lane = lax.broadcasted_iota(jnp.int32, (n_q, cw), 1)

    def cached(ref):
        def chunk(c):
            return jnp.concatenate([ref[pl.ds(2 * c, page, stride=heads), :],
                                    ref[pl.ds(2 * c + 1, page, stride=heads), :]], axis=1).astype(BF16)
        return chunk

    def process(pages, visible):
        zs = [jnp.concatenate(
            [_dot_nt(qbd_ref[c * cr:(c + 1) * cr, :], k_chunk(c)) for c in range(n_chunk)],
            axis=0) * scale + bias_ref[...] for k_chunk, _ in pages]
        log_ws, fsums = _sb_blocks(zs, visible, u_ref[...])
        r = r_ref[...]
        for log_w, fsum, (_, v_chunk) in zip(log_ws, fsums, pages):
            w = _sb_weights(log_w, r, visible)
            for c in range(n_chunk):
                acc_ref[c * cr:(c + 1) * cr, :] += _dot(w[c * cr:(c + 1) * cr, :], v_chunk(c))
            r = r + fsum
        r_ref[...] = r

    @pl.when(step == 0)
    def _():
        r_ref[...] = jnp.zeros_like(r_ref)
        acc_ref[...] = jnp.zeros_like(acc_ref)
        q = q_ref[...]
        for c in range(n_chunk):
            qc = q[:, c * cw:(c + 1) * cw]
            qbd_ref[c * cr:(c + 1) * cr, :] = jnp.concatenate(
                [jnp.where(lane < dh, qc, 0.0), jnp.where(lane >= dh, qc, 0.0)], axis=0).astype(BF16)
        pad = jnp.zeros((page - n_q, cw), F32)
        row = lax.broadcasted_iota(jnp.int32, (n_chunk * cr, page), 0)
        col = lax.broadcasted_iota(jnp.int32, (n_chunk * cr, page), 1)

        def fresh(ref):
            def chunk(c):
                return jnp.concatenate([ref[:, c * cw:(c + 1) * cw], pad], axis=0).astype(BF16)
            return chunk

        process([(fresh(kn_ref), fresh(vn_ref))], col < row % n_q)

    process([(cached(k_ref), cached(v_ref)) for k_ref, v_ref in zip(k_refs, v_refs)], None)

    @pl.when(step == pl.num_programs(1) - 1)
    def _():
        for c in range(n_chunk):
            o_ref[:, c * cw:(c + 1) * cw] = jnp.where(
                lane < dh, acc_ref[c * cr:c * cr + n_q, :], acc_ref[c * cr + n_q:(c + 1) * cr, :])


def sb_attention_sample(q, k_new, v_new, bias, cache_k, cache_v, page_table, layer, heads,
                        pages_per_step=4):
    batch, n_q, width = q.shape
    dh = width // heads
    page = cache_k.shape[2] // heads
    n_pages = page_table.shape[1]
    assert n_q == SUBLANES and page == LANES and n_pages % pages_per_step == 0
    rows = heads * n_q
    bias_rows = jnp.repeat(bias.astype(F32), n_q).reshape(rows, 1)

    def new_spec():
        return pl.BlockSpec((None, n_q, width), lambda b, s, pt: (b, 0, 0))

    def page_spec(slot):
        def index_map(b, s, pt):
            return (layer, pt[b, n_pages - 1 - (s * pages_per_step + slot)], 0, 0)
        return pl.BlockSpec((None, None, page * heads, dh), index_map)

    page_specs = [page_spec(i) for i in range(pages_per_step)]
    grid_spec = pltpu.PrefetchScalarGridSpec(
        num_scalar_prefetch=1,
        grid=(batch, n_pages // pages_per_step),
        in_specs=[new_spec(), new_spec(), new_spec(),
                  pl.BlockSpec((rows, 1), lambda b, s, pt: (0, 0)),
                  pl.BlockSpec((page, page), lambda b, s, pt: (0, 0))] + page_specs + page_specs,
        out_specs=new_spec(),
        scratch_shapes=[pltpu.VMEM((rows, 1), F32), pltpu.VMEM((rows, 2 * dh), F32),
                        pltpu.VMEM((rows, 2 * dh), BF16)],
    )
    return pl.pallas_call(
        functools.partial(_sb_sample_kernel, pages_per_step=pages_per_step, n_q=n_q, dh=dh,
                          scale=dh ** -0.5),
        grid_spec=grid_spec,
        out_shape=jax.ShapeDtypeStruct((batch, n_q, width), F32),
        compiler_params=_params("parallel", "arbitrary"),
        name="sb_attention_sample",
    )(page_table, q, k_new, v_new, bias_rows, _suffix_matrix(page),
      *([cache_k] * pages_per_step), *([cache_v] * pages_per_step))


def _conv_branch_kernel(x_ref, prev_ref, w_ref, b_ref, g_ref, beta_ref, o_ref, ext_ref, y_ref,
                        *, tt, rc, cc):
    t = pl.program_id(1)
    halo = CV_WIDTH - 1
    top = ext_ref.shape[0] - tt
    ch = x_ref.shape[1]

    @pl.when(t == 0)
    def _():
        ext_ref[top - halo:top, :] = prev_ref[...]

    @pl.when(t > 0)
    def _():
        ext_ref[0:top, :] = ext_ref[tt:tt + top, :]

    ext_ref[top:top + tt, :] = x_ref[...]

    for c0 in range(0, ch, cc):
        cs = slice(c0, c0 + cc)
        for r0 in range(0, tt, rc):
            acc = jnp.broadcast_to(b_ref[:, cs], (rc, cc))
            for j in range(CV_WIDTH):
                lo = top - halo + r0 + j
                acc = acc + ext_ref[lo:lo + rc, cs] * w_ref[j:j + 1, cs]
            y_ref[r0:r0 + rc, cs] = acc

    nr = min(tt, 32)
    for r0 in range(0, tt, nr):
        y = y_ref[r0:r0 + nr, :]
        yc = y - jnp.mean(y, axis=-1, keepdims=True)
        var = jnp.mean(yc * yc, axis=-1, keepdims=True)
        z = yc * lax.rsqrt(var + EPS) * g_ref[...] + beta_ref[...]
        o_ref[r0:r0 + nr, :] = (z * jax.nn.sigmoid(z)).astype(o_ref.dtype)


def conv_branch(glu, prev, dw_w, dw_b, ln_g, ln_b, out_dtype, tt=256):
    b, t, ch = glu.shape
    tt = min(tt, t)
    rc = min(tt, 64)
    top = 32
    seq_spec = pl.BlockSpec((None, tt, ch), lambda i, j: (i, j, 0))
    vec_spec = pl.BlockSpec((1, ch), lambda i, j: (0, 0))
    return pl.pallas_call(
        functools.partial(_conv_branch_kernel, tt=tt, rc=rc, cc=256),
        grid=(b, t // tt),
        in_specs=[seq_spec,
                  pl.BlockSpec((None, CV_WIDTH - 1, ch), lambda i, j: (i, 0, 0)),
                  pl.BlockSpec((CV_WIDTH, ch), lambda i, j: (0, 0)),
                  vec_spec, vec_spec, vec_spec],
        out_specs=seq_spec,
        out_shape=jax.ShapeDtypeStruct((b, t, ch), out_dtype),
        scratch_shapes=[pltpu.VMEM((top + tt, ch), F32), pltpu.VMEM((tt, ch), F32)],
        compiler_params=_params("parallel", "arbitrary"),
        name="conv_branch",
    )(glu, prev, dw_w, dw_b.reshape(1, ch), ln_g.reshape(1, ch), ln_b.reshape(1, ch))


def _pool_branch_kernel(x_ref, prev_ref, wg_ref, sc_ref, o_ref, ext_ref, pm_ref, *, tt, rc, pos0):
    t = pl.program_id(1)
    top = ext_ref.shape[0] - tt
    gc = wg_ref.shape[1]

    @pl.when(t == 0)
    def _():
        ext_ref[top - POOL_BUF:top, :] = prev_ref[...]

    @pl.when(t > 0)
    def _():
        ext_ref[0:top, :] = ext_ref[tt:tt + top, :]

    ext_ref[top:top + tt, :] = x_ref[...]

    for g, win in enumerate(POOL_WINDOWS):
        cs = slice(g * gc, (g + 1) * gc)
        for r0 in range(0, tt, rc):
            cur = ext_ref[top + r0:top + r0 + rc, cs]
            acc = cur
            for i in range(1, win):
                acc = acc + ext_ref[top + r0 - i:top + r0 - i + rc, cs]
            pos = pos0 + t * tt + r0 + lax.broadcasted_iota(jnp.int32, (rc, gc), 0)
            count = jnp.minimum(pos + 1, win).astype(F32)
            pm_ref[r0:r0 + rc, cs] = acc / count - cur
        y = _dot(pm_ref[:, cs].astype(BF16), wg_ref[g]) * sc_ref[:, cs]
        o_ref[:, cs] = y.astype(o_ref.dtype)


def pool_branch(x, prev, w_grp, scale, pos0, out_dtype, tt=256):
    b, t, ch = x.shape
    tt = min(tt, t)
    rc = min(tt, 64)
    top = 16
    seq_spec = pl.BlockSpec((None, tt, ch), lambda i, j: (i, j, 0))
    return pl.pallas_call(
        functools.partial(_pool_branch_kernel, tt=tt, rc=rc, pos0=pos0),
        grid=(b, t // tt),
        in_specs=[seq_spec,
                  pl.BlockSpec((None, POOL_BUF, ch), lambda i, j: (i, 0, 0)),
                  pl.BlockSpec(w_grp.shape, lambda i, j: (0, 0, 0)),
                  pl.BlockSpec((1, ch), lambda i, j: (0, 0))],
        out_specs=seq_spec,
        out_shape=jax.ShapeDtypeStruct((b, t, ch), out_dtype),
        scratch_shapes=[pltpu.VMEM((top + tt, ch), F32), pltpu.VMEM((tt, ch), F32)],
        compiler_params=_params("parallel", "arbitrary"),
        name="pool_branch",
    )(x, prev, w_grp, scale.reshape(1, ch))


def _cross_attn_kernel(q_ref, mk_ref, mv_ref, o_ref, *, heads, scale):
    dh = q_ref.shape[1] // heads
    for h in range(heads):
        cs = slice(h * dh, (h + 1) * dh)
        s = _dot_nt(q_ref[:, cs].astype(BF16), mk_ref[:, cs].astype(BF16)) * scale
        e = jnp.exp(s - jnp.max(s, axis=-1, keepdims=True))
        a = e / jnp.sum(e, axis=-1, keepdims=True)
        o_ref[:, cs] = _dot(a.astype(BF16), mv_ref[:, cs].astype(BF16)).astype(o_ref.dtype)


def cross_attention(q, mem_k, mem_v, heads, out_dtype, tq=512):
    b, t, width = q.shape
    n_mem = mem_k.shape[1]
    tq = min(tq, t)
    q_spec = pl.BlockSpec((None, tq, width), lambda i, j: (i, j, 0))
    mem_spec = pl.BlockSpec((None, n_mem, width), lambda i, j: (i, 0, 0))
    return pl.pallas_call(
        functools.partial(_cross_attn_kernel, heads=heads, scale=(width // heads) ** -0.5),
        grid=(b, t // tq),
        in_specs=[q_spec, mem_spec, mem_spec],
        out_specs=q_spec,
        out_shape=jax.ShapeDtypeStruct((b, t, width), out_dtype),
        compiler_params=_params("parallel", "parallel"),
        name="cross_attention",
    )(q, mem_k, mem_v)


def _ffn_up_kernel(x_ref, wv_ref, wg_ref, cwv_ref, cwg_ref, cbv_ref, cbg_ref, *refs,
                   tm, rc, tiles_per_seq, seq_len):
    pad = SUBLANES
    if tiles_per_seq:
        act_ref, tailv_ref, tailg_ref, extv_ref, extg_ref = refs
        fixes = (None, None)
    else:
        f1v_ref, f2v_ref, f1g_ref, f2g_ref, act_ref, tailv_ref, tailg_ref, extv_ref, extg_ref = refs
        fixes = ((f1v_ref, f2v_ref), (f1g_ref, f2g_ref))
    i = pl.program_id(1)
    first = i % max(tiles_per_seq, 1) == 0

    for ext_ref in (extv_ref, extg_ref):
        @pl.when(first)
        def _():
            ext_ref[0:pad, :] = jnp.zeros((pad, ext_ref.shape[1]), F32)

        @pl.when(jnp.logical_not(first))
        def _():
            ext_ref[0:pad, :] = ext_ref[tm:tm + pad, :]

    def conv(u, r0, cw_ref, cb_ref, tail_ref, ext_ref, fix):
        if not tiles_per_seq:
            tail_ref[r0:r0 + rc, :] = u
        elif r0 + rc == tm:
            tail_ref[...] = u[rc - pad:, :]
        ext_ref[pad + r0:pad + r0 + rc, :] = u
        u1 = ext_ref[pad - 1 + r0:pad - 1 + r0 + rc, :]
        u2 = ext_ref[pad - 2 + r0:pad - 2 + r0 + rc, :]
        if fix is not None:
            t_in_seq = (r0 + lax.broadcasted_iota(jnp.int32, u.shape, 0)) % seq_len
            u1 = jnp.where(t_in_seq < 1, fix[0][r0:r0 + rc, :], u1)
            u2 = jnp.where(t_in_seq < 2, fix[1][r0:r0 + rc, :], u2)
        return cb_ref[...] + cw_ref[0:1, :] * u2 + cw_ref[1:2, :] * u1 + cw_ref[2:3, :] * u

    def epilogue(r0, u_val, u_gate):
        val = conv(u_val, r0, cwv_ref, cbv_ref, tailv_ref, extv_ref, fixes[0])
        gate = conv(u_gate, r0, cwg_ref, cbg_ref, tailg_ref, extg_ref, fixes[1])
        act_ref[r0:r0 + rc, :] = (val * (gate * jax.nn.sigmoid(gate))).astype(act_ref.dtype)

    pending = None
    for r0 in range(0, tm, rc):
        x = x_ref[r0:r0 + rc, :].astype(BF16)
        chunk = (r0, _dot(x, wv_ref[...]), _dot(x, wg_ref[...]))
        if pending is not None:
            epilogue(*pending)
        pending = chunk
    epilogue(*pending)


def ffn_up(x, w_up, dw_w, dw_b, seq_len, prev, tm=1024, tn=512):
    m, d = x.shape
    f = w_up.shape[1] // 2
    nb = f // tn
    tm = min(tm, m)
    n_tiles = m // tm
    x_spec = pl.BlockSpec((tm, d), lambda j, i: (i, 0))
    col = lambda rows, off: pl.BlockSpec((rows, tn), lambda j, i: (0, j + off))
    tile = lambda off: pl.BlockSpec((tm, tn), lambda j, i: (i, j + off))
    in_specs = [x_spec, col(d, 0), col(d, nb), col(FFN_CONV, 0), col(FFN_CONV, nb), col(1, 0), col(1, nb)]
    args = [x, w_up, w_up, dw_w, dw_w, dw_b.reshape(1, 2 * f), dw_b.reshape(1, 2 * f)]
    if prev is None:
        assert seq_len % tm == 0
        tiles_per_seq = seq_len // tm
        tail_rows = SUBLANES
    else:
        assert n_tiles == 1 and seq_len >= FFN_CONV - 1
        tiles_per_seq = 0
        tail_rows = tm
        batch = m // seq_len
        zeros = jnp.zeros((batch, seq_len, 2 * f), F32)
        fix1 = zeros.at[:, 0].set(prev[:, 1]).reshape(m, 2 * f)
        fix2 = zeros.at[:, 0].set(prev[:, 0]).at[:, 1].set(prev[:, 1]).reshape(m, 2 * f)
        in_specs += [tile(0), tile(0), tile(nb), tile(nb)]
        args += [fix1, fix2, fix1, fix2]
    tail_spec = pl.BlockSpec((None, tail_rows, tn), lambda j, i: (i, 0, j))
    tail_shape = jax.ShapeDtypeStruct((n_tiles, tail_rows, f), F32)
    return pl.pallas_call(
        functools.partial(_ffn_up_kernel, tm=tm, rc=min(tm, 128), tiles_per_seq=tiles_per_seq,
                          seq_len=seq_len),
        grid=(nb, n_tiles),
        in_specs=in_specs,
        out_specs=[pl.BlockSpec((tm, tn), lambda j, i: (i, j)), tail_spec, tail_spec],
        out_shape=[jax.ShapeDtypeStruct((m, f), BF16), tail_shape, tail_shape],
        scratch_shapes=[pltpu.VMEM((SUBLANES + tm, tn), F32), pltpu.VMEM((SUBLANES + tm, tn), F32)],
        compiler_params=_params("parallel", "arbitrary"),
        name="ffn_up",
    )(*args)


def _pad_halves(a, half, half_pad):
    widths = [(0, 0)] * (a.ndim - 1) + [(0, half_pad - half)]
    return jnp.concatenate([jnp.pad(a[..., :half], widths), jnp.pad(a[..., half:], widths)], axis=-1)


def _layer_weights(l, p, d_ff_pad):
    d = p['w_in'].shape[1]
    sbw = p['w_sb_out'].shape[1]
    cvc = p['w_cv_out'].shape[1]
    plc = p['w_pool_out'].shape[1]
    w_in = p['w_in'][l]
    offs = [0, sbw, 2 * sbw, 3 * sbw, 3 * sbw + cvc, 3 * sbw + 2 * cvc, 3 * sbw + 2 * cvc + plc]
    names = ['w_q', 'w_k', 'w_v', 'w_cva', 'w_cvb', 'w_pool_in']
    w = {n: w_in[:, a:b].astype(BF16) for n, a, b in zip(names, offs[:-1], offs[1:])}
    w['w_gates'] = w_in[:, offs[-1]:].astype(BF16)
    for n in ('w_sb_out', 'w_cv_out', 'w_pool_out', 'w_mix_out', 'w_xq', 'w_xo', 'w_pool_grp'):
        w[n] = p[n][l].astype(BF16)
    xw = p['w_xkv'].shape[2] // 2
    w['w_xk'] = p['w_xkv'][l][:, :xw].astype(BF16)
    w['w_xv'] = p['w_xkv'][l][:, xw:].astype(BF16)
    d_ff = p['w_down'].shape[1]
    w['w_up'] = _pad_halves(p['w_up'][l], d_ff, d_ff_pad).astype(BF16)
    w['ffn_dw_w'] = _pad_halves(p['ffn_dw_w'][l], d_ff, d_ff_pad)
    w['ffn_dw_b'] = _pad_halves(p['ffn_dw_b'][l], d_ff, d_ff_pad)
    w['w_down'] = jnp.pad(p['w_down'][l], ((0, d_ff_pad - d_ff), (0, 0))).astype(BF16)
    for n in ('g_mix', 'sb_bias', 'cv_dw_w', 'cv_dw_b', 'cv_ln_g', 'cv_ln_b', 'pool_scale',
              'g_xattn', 'g_ffn'):
        w[n] = p[n][l]
    return w


def _state(prev, cur, n):
    t = cur.shape[1]
    if t >= n:
        return cur[:, t - n:]
    return jnp.concatenate([prev[:, prev.shape[1] - (n - t):], cur], axis=1)


def _decoder_layer(x, batch, seq, w, mem_k, mem_v, xa_heads, conv_prev, pool_prev, ffn_prev, past, d_ff):
    m, d = x.shape
    heads = w['sb_bias'].shape[0]
    prompt = past is None
    act_dt = BF16 if prompt else F32
    three = lambda a: a.reshape(batch, seq, a.shape[-1])
    two = lambda a: a.reshape(m, a.shape[-1])

    h = rmsnorm(x, w['g_mix'], BF16)
    glu = linear_glu(h, w['w_cva'], w['w_cvb'])
    pool_in, = linear(h, w['w_pool_in'], (F32,))
    if prompt:
        q, = linear(h, w['w_q'], (BF16,))
        k, kb = linear(h, w['w_k'], (F32, BF16))
        v, vb = linear(h, w['w_v'], (F32, BF16))
        attn = sb_attention_prompt(q, kb, vb, w['sb_bias'], batch, seq, heads)
        pos0 = 0
    else:
        q, = linear(h, w['w_q'], (F32,))
        k, = linear(h, w['w_k'], (F32,))
        v, = linear(h, w['w_v'], (F32,))
        cache_k, cache_v, page_table, layer = past
        attn = two(sb_attention_sample(three(q), three(k), three(v), w['sb_bias'],
                                       cache_k, cache_v, page_table, layer, heads))
        pos0 = page_table.shape[1] * (cache_k.shape[2] // heads)
    conv = conv_branch(three(glu), conv_prev, w['cv_dw_w'], w['cv_dw_b'], w['cv_ln_g'], w['cv_ln_b'], act_dt)
    pool = pool_branch(three(pool_in), pool_prev, w['w_pool_grp'], w['pool_scale'], pos0, act_dt)
    merged = gated_merge(h, attn, two(conv), two(pool), w['w_gates'],
                         w['w_sb_out'], w['w_cv_out'], w['w_pool_out'])
    x = linear_res(merged, w['w_mix_out'], x, tm=1024, tn=512)

    h = rmsnorm(x, w['g_xattn'], BF16)
    qx, = linear(h, w['w_xq'], (act_dt,))
    xa = cross_attention(three(qx), mem_k, mem_v, xa_heads, act_dt)
    x = linear_res(two(xa), w['w_xo'], x, tm=1024, tn=1024)

    h = rmsnorm(x, w['g_ffn'], BF16)
    d_ff_pad = w['w_down'].shape[0]
    act, tail_v, tail_g = ffn_up(h, w['w_up'], w['ffn_dw_w'], w['ffn_dw_b'], seq,
                                 None if prompt else _pad_halves(ffn_prev, d_ff, d_ff_pad))
    x = linear_res(act, w['w_down'], x)

    tails = []
    for tail in (tail_v, tail_g):
        if prompt:
            per_seq = tail.shape[0] // batch
            tails.append(tail[per_seq - 1::per_seq, SUBLANES - (FFN_CONV - 1):, :d_ff])
        else:
            tails.append(tail.reshape(batch, seq, -1)[:, seq - (FFN_CONV - 1):, :d_ff])
    ffn_state = jnp.concatenate(tails, axis=-1)
    conv_state = _state(conv_prev, three(glu), CV_WIDTH - 1)
    pool_state = _state(pool_prev, three(pool_in), POOL_BUF)
    return x, k, v, conv_state, pool_state, ffn_state


def kernel(x_prompt, x_sample, mem_prompt, cache_sb_k, cache_sb_v, cache_mem_k, cache_mem_v, state_conv, state_pool, state_ffn, page_table, g_mix, w_in, sb_bias, w_sb_out, cv_dw_w, cv_dw_b, cv_ln_g, cv_ln_b, w_cv_out, w_pool_grp, pool_scale, w_pool_out, w_mix_out, g_xattn, g_mem, w_xq, w_xkv, w_xo, g_ffn, w_up, ffn_dw_w, ffn_dw_b, w_down, g_final):
    p = dict(g_mix=g_mix, w_in=w_in, sb_bias=sb_bias, w_sb_out=w_sb_out, cv_dw_w=cv_dw_w,
             cv_dw_b=cv_dw_b, cv_ln_g=cv_ln_g, cv_ln_b=cv_ln_b, w_cv_out=w_cv_out,
             w_pool_grp=w_pool_grp, pool_scale=pool_scale, w_pool_out=w_pool_out,
             w_mix_out=w_mix_out, g_xattn=g_xattn, w_xq=w_xq, w_xkv=w_xkv, w_xo=w_xo,
             g_ffn=g_ffn, w_up=w_up, ffn_dw_w=ffn_dw_w, ffn_dw_b=ffn_dw_b, w_down=w_down)
    depth = w_in.shape[0]
    bp, tp, d = x_prompt.shape
    bs, ts, _ = x_sample.shape
    heads, dh = cache_sb_k.shape[3], cache_sb_k.shape[4]
    xa_heads, xa_dh = cache_mem_k.shape[3], cache_mem_k.shape[4]
    n_mem = mem_prompt.shape[1]
    d_ff = w_down.shape[1]
    ffn_tn = 512
    d_ff_pad = -(-d_ff // ffn_tn) * ffn_tn
    cache_k = cache_sb_k.reshape(cache_sb_k.shape[:2] + (-1, dh))
    cache_v = cache_sb_v.reshape(cache_sb_v.shape[:2] + (-1, dh))
    mem_flat = mem_prompt.reshape(bp * n_mem, d)

    xp = x_prompt.reshape(bp * tp, d)
    xs = x_sample.reshape(bs * ts, d)
    conv0 = jnp.zeros((bp,) + state_conv.shape[2:], F32)
    pool0 = jnp.zeros((bp,) + state_pool.shape[2:], F32)
    outs_p, outs_s = [], []
    for l in range(depth):
        w = _layer_weights(l, p, d_ff_pad)
        mem_h = rmsnorm(mem_flat, g_mem[l], BF16)
        mk, = linear(mem_h, w['w_xk'], (F32,))
        mv, = linear(mem_h, w['w_xv'], (F32,))
        mk3 = mk.reshape(bp, n_mem, xa_heads * xa_dh)
        mv3 = mv.reshape(bp, n_mem, xa_heads * xa_dh)
        xp, kp, vp, cp, pp, fp = _decoder_layer(xp, bp, tp, w, mk3, mv3, xa_heads, conv0, pool0, None,
                                                None, d_ff)
        xs, ks, vs, cs, ps, fs = _decoder_layer(
            xs, bs, ts, w, cache_mem_k[l].reshape(bs, n_mem, -1), cache_mem_v[l].reshape(bs, n_mem, -1),
            xa_heads, state_conv[l], state_pool[l], state_ffn[l], (cache_k, cache_v, page_table, l), d_ff)
        outs_p.append((kp.reshape(bp, tp, heads, dh), vp.reshape(bp, tp, heads, dh),
                       mk.reshape(bp, n_mem, xa_heads, xa_dh), mv.reshape(bp, n_mem, xa_heads, xa_dh),
                       cp, pp, fp))
        outs_s.append((ks.reshape(bs, ts, heads, dh), vs.reshape(bs, ts, heads, dh), cs, ps, fs))
    y_prompt = rmsnorm(xp, g_final, F32).reshape(bp, tp, d)
    y_sample = rmsnorm(xs, g_final, F32).reshape(bs, ts, d)
    stack = lambda rows, i: jnp.stack([r[i] for r in rows])
    return ((y_prompt, y_sample) + tuple(stack(outs_p, i) for i in range(7))
            + tuple(stack(outs_s, i) for i in range(5)))
```

```python
import functools
from typing import NamedTuple

import jax
import jax.numpy as jnp
from jax import lax
from jax.experimental import pallas as pl
from jax.experimental.pallas import tpu as pltpu

F32 = jnp.float32
BF16 = jnp.bfloat16

EPS = 1e-6
CV_WIDTH = 31
POOL_WINDOWS = (2, 4, 8, 16)
POOL_BUF = max(POOL_WINDOWS) - 1
FFN_CONV = 3

VMEM_LIMIT_BYTES = 60 * 2**20
LANES = 128
SUBLANES = 8
MXU_DIM = 256

_NT_DIMS = (((1,), (1,)), ((), ()))


def _params(*semantics):
    return pltpu.CompilerParams(dimension_semantics=semantics, vmem_limit_bytes=VMEM_LIMIT_BYTES)


def _dot(a, b):
    return jnp.dot(a, b, preferred_element_type=F32)


def _dot_nt(a, b):
    return lax.dot_general(a, b, _NT_DIMS, preferred_element_type=F32)


def _rmsnorm_kernel(x_ref, g_ref, o_ref):
    x = x_ref[...]
    ms = jnp.mean(x * x, axis=-1, keepdims=True)
    o_ref[...] = (x * lax.rsqrt(ms + EPS) * g_ref[...]).astype(o_ref.dtype)


def rmsnorm(x, g, out_dtype):
    m, d = x.shape
    tr = min(m, 256)
    return pl.pallas_call(
        _rmsnorm_kernel,
        grid=(m // tr,),
        in_specs=[pl.BlockSpec((tr, d), lambda i: (i, 0)),
                  pl.BlockSpec((1, d), lambda i: (0, 0))],
        out_specs=pl.BlockSpec((tr, d), lambda i: (i, 0)),
        out_shape=jax.ShapeDtypeStruct((m, d), out_dtype),
        compiler_params=_params("parallel"),
        name="rmsnorm",
    )(x, g.reshape(1, d))


def _linear_kernel(x_ref, w_ref, *o_refs):
    acc = _dot(x_ref[...].astype(BF16), w_ref[...])
    for o_ref in o_refs:
        o_ref[...] = acc.astype(o_ref.dtype)


class Weight(NamedTuple):
    stack: jax.Array
    layer: int
    col0: int
    width: int

    @property
    def k(self):
        return self.stack.shape[1]

    def spec(self, tn, col_axis, block_off=0):
        assert self.col0 % tn == 0 and self.width % tn == 0
        first = self.col0 // tn + block_off
        layer = self.layer
        return pl.BlockSpec((None, self.k, tn), lambda *g: (layer, 0, first + g[col_axis]))


def weight(stack, layer, col0=0, width=None):
    return Weight(stack, layer, col0, stack.shape[2] - col0 if width is None else width)


def linear(x, w, out_dtypes, tm=1024, tn=1024):
    m, k = x.shape
    n = w.width
    tm, tn = min(tm, m), min(tn, n)
    outs = pl.pallas_call(
        _linear_kernel,
        grid=(n // tn, m // tm),
        in_specs=[pl.BlockSpec((tm, k), lambda j, i: (i, 0)), w.spec(tn, 0)],
        out_specs=[pl.BlockSpec((tm, tn), lambda j, i: (i, j)) for _ in out_dtypes],
        out_shape=[jax.ShapeDtypeStruct((m, n), dt) for dt in out_dtypes],
        compiler_params=_params("parallel", "parallel"),
        name="linear",
    )(x, w.stack)
    return outs


def _linear_glu_kernel(x_ref, wa_ref, wb_ref, o_ref):
    x = x_ref[...].astype(BF16)
    o_ref[...] = _dot(x, wa_ref[...]) * jax.nn.sigmoid(_dot(x, wb_ref[...]))


def linear_glu(x, wa, wb, tm=1024, tn=512):
    m, k = x.shape
    n = wa.width
    tm, tn = min(tm, m), min(tn, n)
    return pl.pallas_call(
        _linear_glu_kernel,
        grid=(n // tn, m // tm),
        in_specs=[pl.BlockSpec((tm, k), lambda j, i: (i, 0)), wa.spec(tn, 0), wb.spec(tn, 0)],
        out_specs=pl.BlockSpec((tm, tn), lambda j, i: (i, j)),
        out_shape=jax.ShapeDtypeStruct((m, n), F32),
        compiler_params=_params("parallel", "parallel"),
        name="linear_glu",
    )(x, wa.stack, wb.stack)


def _linear_res_kernel(x_ref, w_ref, r_ref, o_ref):
    o_ref[...] = r_ref[...] + _dot(x_ref[...].astype(BF16), w_ref[...])


def linear_res(x, w, res, tm=512, tn=512):
    m, k = x.shape
    n = w.width
    tm, tn = min(tm, m), min(tn, n)
    return pl.pallas_call(
        _linear_res_kernel,
        grid=(n // tn, m // tm),
        in_specs=[pl.BlockSpec((tm, k), lambda j, i: (i, 0)), w.spec(tn, 0),
                  pl.BlockSpec((tm, tn), lambda j, i: (i, j))],
        out_specs=pl.BlockSpec((tm, tn), lambda j, i: (i, j)),
        out_shape=jax.ShapeDtypeStruct((m, n), F32),
        compiler_params=_params("parallel", "parallel"),
        name="linear_res",
    )(x, w.stack, res)


def _merge_kernel(h_ref, a_ref, c_ref, p_ref, wg0_ref, wg1_ref, wg2_ref,
                  wsb_ref, wcv_ref, wpl_ref, o_ref):
    h = h_ref[...].astype(BF16)

    def gated(wg_ref, y_ref, w_ref):
        gate = jax.nn.sigmoid(_dot(h, wg_ref[...]))
        return gate * _dot(y_ref[...].astype(BF16), w_ref[...])

    merged = (gated(wg0_ref, a_ref, wsb_ref) + gated(wg1_ref, c_ref, wcv_ref)
              + gated(wg2_ref, p_ref, wpl_ref))
    o_ref[...] = merged.astype(o_ref.dtype)


def gated_merge(h, attn, conv, pool, w_gates, w_sb_out, w_cv_out, w_pool_out, tm=512, tn=256):
    m, d = h.shape
    n = w_sb_out.width
    tm = min(tm, m)
    nb = n // tn

    def rows(a):
        return pl.BlockSpec((tm, a.shape[1]), lambda i, j: (i, 0))

    return pl.pallas_call(
        _merge_kernel,
        grid=(m // tm, nb),
        in_specs=[rows(h), rows(attn), rows(conv), rows(pool),
                  w_gates.spec(tn, 1), w_gates.spec(tn, 1, nb), w_gates.spec(tn, 1, 2 * nb),
                  w_sb_out.spec(tn, 1), w_cv_out.spec(tn, 1), w_pool_out.spec(tn, 1)],
        out_specs=pl.BlockSpec((tm, tn), lambda i, j: (i, j)),
        out_shape=jax.ShapeDtypeStruct((m, n), BF16),
        compiler_params=_params("parallel", "parallel"),
        name="gated_merge",
    )(h, attn, conv, pool, w_gates.stack, w_gates.stack, w_gates.stack,
      w_sb_out.stack, w_cv_out.stack, w_pool_out.stack)


def _fail_nats(z):
    return jnp.maximum(z, 0.0) + jnp.log(1.0 + jnp.exp(-jnp.abs(z)))


def _suffix_sum(f, u2):
    f_hi = f.astype(BF16)
    f_lo = (f - f_hi.astype(F32)).astype(BF16)
    return _dot(jnp.concatenate([f_hi, f_lo], axis=1), u2)


def _sb_blocks(zs, visible, u2):
    fs = [_fail_nats(z) for z in zs]
    log_hits = [z - f for z, f in zip(zs, fs)]
    if visible is not None:
        fs = [jnp.where(visible, f, 0.0) for f in fs]
    sufs = [_suffix_sum(f, u2) for f in fs]
    return ([lh - s for lh, s in zip(log_hits, sufs)],
            [jnp.sum(f, axis=-1, keepdims=True) for f in fs])


def _sb_weights(log_w, r, visible):
    w = jnp.exp(log_w - r)
    if visible is not None:
        w = jnp.where(visible, w, 0.0)
    return w.astype(BF16)


def _sb_prompt_kernel(q_ref, k_ref, v_ref, bias_ref, u_ref, o_ref, r_ref, acc_ref, *, tq, dh, scale):
    i = pl.program_id(2)
    u = u_ref[...]
    r_ref[...] = jnp.zeros_like(r_ref)
    acc_ref[...] = jnp.zeros_like(acc_ref)

    def block(kb, visible):
        start = pl.multiple_of(kb * tq, tq)
        cols = [slice(hh * dh, (hh + 1) * dh) for hh in range(q_ref.shape[1] // dh)]
        zs = [_dot_nt(q_ref[:, cs], k_ref[pl.ds(start, tq), cs]) * scale + bias_ref[hh:hh + 1, :]
              for hh, cs in enumerate(cols)]
        log_ws, fsums = _sb_blocks(zs, visible, u)
        for hh, cs in enumerate(cols):
            w = _sb_weights(log_ws[hh], r_ref[hh], visible)
            acc_ref[:, cs] += _dot(w, v_ref[pl.ds(start, tq), cs])
            r_ref[hh] += fsums[hh]

    row = lax.broadcasted_iota(jnp.int32, (tq, tq), 0)
    col = lax.broadcasted_iota(jnp.int32, (tq, tq), 1)
    block(i, col < row)

    def older(n, carry):
        block(i - 1 - n, None)
        return carry

    lax.fori_loop(0, i, older, 0)
    o_ref[...] = acc_ref[...].astype(o_ref.dtype)


def _suffix_matrix(n):
    j = lax.broadcasted_iota(jnp.int32, (2 * n, n), 0) % n
    s = lax.broadcasted_iota(jnp.int32, (2 * n, n), 1)
    return (j > s).astype(BF16)


def sb_attention_prompt(q, k, v, bias, batch, seq, heads, tq=256, heads_per_step=8):
    m, width = q.shape
    dh = width // heads
    nq = seq // tq
    hps = heads_per_step
    bias_b = jnp.broadcast_to(bias.astype(F32).reshape(heads // hps, hps, 1), (heads // hps, hps, tq))
    kv_spec = pl.BlockSpec((seq, hps * dh), lambda b, h, i: (b, h))
    q_spec = pl.BlockSpec((tq, hps * dh), lambda b, h, i: (b * nq + i, h))
    return pl.pallas_call(
        functools.partial(_sb_prompt_kernel, tq=tq, dh=dh, scale=dh ** -0.5),
        grid=(batch, heads // hps, nq),
        in_specs=[q_spec, kv_spec, kv_spec,
                  pl.BlockSpec((None, hps, tq), lambda b, h, i: (h, 0, 0)),
                  pl.BlockSpec((2 * tq, tq), lambda b, h, i: (0, 0))],
        out_specs=q_spec,
        out_shape=jax.ShapeDtypeStruct((m, width), BF16),
        scratch_shapes=[pltpu.VMEM((hps, tq, 1), F32), pltpu.VMEM((tq, hps * dh), F32)],
        compiler_params=_params("parallel", "parallel", "parallel"),
        name="sb_attention_prompt",
    )(q, k, v, bias_b, _suffix_matrix(tq))


def _sb_sample_kernel(pt_ref, q_ref, kn_ref, vn_ref, bias_ref, u_ref, *refs,
                      pages_per_step, n_q, dh, scale):
    del pt_ref
    k_refs = refs[:pages_per_step]
    v_refs = refs[pages_per_step:2 * pages_per_step]
    o_ref, r_ref, acc_ref, qbd_ref = refs[2 * pages_per_step:]
    step = pl.program_id(1)
    width = q_ref.shape[1]
    heads = width // dh
    page = k_refs[0].shape[0] // heads
    cw = 2 * dh
    cr = 2 * n_q
    n_chunk = width // cw
    lane = lax.broadcasted_iota(jnp.int32, (n_q, cw), 1)

    def cached(ref):
        def chunk(c):
            return jnp.concatenate([ref[pl.ds(2 * c, page, stride=heads), :],
                                    ref[pl.ds(2 * c + 1, page, stride=heads), :]], axis=1).astype(BF16)
        return chunk

    def process(pages, visible):
        zs = [jnp.concatenate(
            [_dot_nt(qbd_ref[c * cr:(c + 1) * cr, :], k_chunk(c)) for c in range(n_chunk)],
            axis=0) * scale + bias_ref[...] for k_chunk, _ in pages]
        log_ws, fsums = _sb_blocks(zs, visible, u_ref[...])
        r = r_ref[...]
        for log_w, fsum, (_, v_chunk) in zip(log_ws, fsums, pages):
            w = _sb_weights(log_w, r, visible)
            for c in range(n_chunk):
                acc_ref[c * cr:(c + 1) * cr, :] += _dot(w[c * cr:(c + 1) * cr, :], v_chunk(c))
            r = r + fsum
        r_ref[...] = r

    @pl.when(step == 0)
    def _():
        r_ref[...] = jnp.zeros_like(r_ref)
        acc_ref[...] = jnp.zeros_like(acc_ref)
        q = q_ref[...]
        for c in range(n_chunk):
            qc = q[:, c * cw:(c + 1) * cw]
            qbd_ref[c * cr:(c + 1) * cr, :] = jnp.concatenate(
                [jnp.where(lane < dh, qc, 0.0), jnp.where(lane >= dh, qc, 0.0)], axis=0).astype(BF16)
        pad = jnp.zeros((page - n_q, cw), F32)
        row = lax.broadcasted_iota(jnp.int32, (n_chunk * cr, page), 0)
        col = lax.broadcasted_iota(jnp.int32, (n_chunk * cr, page), 1)

        def fresh(ref):
            def chunk(c):
                return jnp.concatenate([ref[:, c * cw:(c + 1) * cw], pad], axis=0).astype(BF16)
            return chunk

        process([(fresh(kn_ref), fresh(vn_ref))], col < row % n_q)

    process([(cached(k_ref), cached(v_ref)) for k_ref, v_ref in zip(k_refs, v_refs)], None)

    @pl.when(step == pl.num_programs(1) - 1)
    def _():
        for c in range(n_chunk):
            o_ref[:, c * cw:(c + 1) * cw] = jnp.where(
                lane < dh, acc_ref[c * cr:c * cr + n_q, :], acc_ref[c * cr + n_q:(c + 1) * cr, :])


def sb_attention_sample(q, k_new, v_new, bias, cache_k, cache_v, page_table, layer, heads,
                        pages_per_step=4):
    batch, n_q, width = q.shape
    dh = width // heads
    page = cache_k.shape[2] // heads
    n_pages = page_table.shape[1]
    assert n_q == SUBLANES and page == LANES and n_pages % pages_per_step == 0
    rows = heads * n_q
    bias_rows = jnp.repeat(bias.astype(F32), n_q).reshape(rows, 1)

    def new_spec():
        return pl.BlockSpec((None, n_q, width), lambda b, s, pt: (b, 0, 0))

    def page_spec(slot):
        def index_map(b, s, pt):
            return (layer, pt[b, n_pages - 1 - (s * pages_per_step + slot)], 0, 0)
        return pl.BlockSpec((None, None, page * heads, dh), index_map)

    page_specs = [page_spec(i) for i in range(pages_per_step)]
    grid_spec = pltpu.PrefetchScalarGridSpec(
        num_scalar_prefetch=1,
        grid=(batch, n_pages // pages_per_step),
        in_specs=[new_spec(), new_spec(), new_spec(),
                  pl.BlockSpec((rows, 1), lambda b, s, pt: (0, 0)),
                  pl.BlockSpec((2 * page, page), lambda b, s, pt: (0, 0))] + page_specs + page_specs,
        out_specs=new_spec(),
        scratch_shapes=[pltpu.VMEM((rows, 1), F32), pltpu.VMEM((rows, 2 * dh), F32),
                        pltpu.VMEM((rows, 2 * dh), BF16)],
    )
    return pl.pallas_call(
        functools.partial(_sb_sample_kernel, pages_per_step=pages_per_step, n_q=n_q, dh=dh,
                          scale=dh ** -0.5),
        grid_spec=grid_spec,
        out_shape=jax.ShapeDtypeStruct((batch, n_q, width), F32),
        compiler_params=_params("parallel", "arbitrary"),
        name="sb_attention_sample",
    )(page_table, q, k_new, v_new, bias_rows, _suffix_matrix(page),
      *([cache_k] * pages_per_step), *([cache_v] * pages_per_step))


def _conv_branch_kernel(x_ref, prev_ref, w_ref, b_ref, g_ref, beta_ref, o_ref, ext_ref, sh_ref, y_ref,
                        *, tt, rc, cc):
    t = pl.program_id(1)
    halo = CV_WIDTH - 1
    top = ext_ref.shape[0] - tt
    ch = x_ref.shape[1]

    @pl.when(t == 0)
    def _():
        ext_ref[top - halo:top, :] = prev_ref[...]

    @pl.when(t > 0)
    def _():
        ext_ref[0:top, :] = ext_ref[tt:tt + top, :]

    ext_ref[top:top + tt, :] = x_ref[...]
    n_ext = top + tt
    for s in range(1, SUBLANES):
        sh_ref[s, 0:n_ext - SUBLANES, :] = ext_ref[s:s + n_ext - SUBLANES, :]

    def rows(lo):
        s = lo % SUBLANES
        if s == 0:
            return ext_ref, lo
        return sh_ref.at[s], lo - s

    for c0 in range(0, ch, cc):
        cs = slice(c0, c0 + cc)
        for r0 in range(0, tt, rc):
            acc = jnp.broadcast_to(b_ref[:, cs], (rc, cc))
            for j in range(CV_WIDTH):
                src, lo = rows(top - halo + r0 + j)
                acc = acc + src[lo:lo + rc, cs] * w_ref[j:j + 1, cs]
            y_ref[r0:r0 + rc, cs] = acc

    nr = min(tt, 32)
    for r0 in range(0, tt, nr):
        y = y_ref[r0:r0 + nr, :]
        yc = y - jnp.mean(y, axis=-1, keepdims=True)
        var = jnp.mean(yc * yc, axis=-1, keepdims=True)
        z = yc * lax.rsqrt(var + EPS) * g_ref[...] + beta_ref[...]
        o_ref[r0:r0 + nr, :] = (z * jax.nn.sigmoid(z)).astype(o_ref.dtype)


def conv_branch(glu, prev, dw_w, dw_b, ln_g, ln_b, out_dtype, tt=256):
    b, t, ch = glu.shape
    tt = min(tt, t)
    rc = min(tt, 64)
    top = 32
    seq_spec = pl.BlockSpec((None, tt, ch), lambda i, j: (i, j, 0))
    vec_spec = pl.BlockSpec((1, ch), lambda i, j: (0, 0))
    return pl.pallas_call(
        functools.partial(_conv_branch_kernel, tt=tt, rc=rc, cc=256),
        grid=(b, t // tt),
        in_specs=[seq_spec,
                  pl.BlockSpec((None, CV_WIDTH - 1, ch), lambda i, j: (i, 0, 0)),
                  pl.BlockSpec((CV_WIDTH, ch), lambda i, j: (0, 0)),
                  vec_spec, vec_spec, vec_spec],
        out_specs=seq_spec,
        out_shape=jax.ShapeDtypeStruct((b, t, ch), out_dtype),
        scratch_shapes=[pltpu.VMEM((top + tt, ch), F32), pltpu.VMEM((SUBLANES, top + tt, ch), F32),
                        pltpu.VMEM((tt, ch), F32)],
        compiler_params=_params("parallel", "arbitrary"),
        name="conv_branch",
    )(glu, prev, dw_w, dw_b.reshape(1, ch), ln_g.reshape(1, ch), ln_b.reshape(1, ch))


def _pool_branch_kernel(x_ref, prev_ref, wg_ref, sc_ref, o_ref, ext_ref, pm_ref, *, tt, rc, pos0):
    t = pl.program_id(1)
    top = ext_ref.shape[0] - tt
    gc = wg_ref.shape[1]

    @pl.when(t == 0)
    def _():
        ext_ref[top - POOL_BUF:top, :] = prev_ref[...]

    @pl.when(t > 0)
    def _():
        ext_ref[0:top, :] = ext_ref[tt:tt + top, :]

    ext_ref[top:top + tt, :] = x_ref[...]

    for g, win in enumerate(POOL_WINDOWS):
        cs = slice(g * gc, (g + 1) * gc)
        for r0 in range(0, tt, rc):
            cur = ext_ref[top + r0:top + r0 + rc, cs]
            acc = cur
            for i in range(1, win):
                acc = acc + ext_ref[top + r0 - i:top + r0 - i + rc, cs]
            pos = pos0 + t * tt + r0 + lax.broadcasted_iota(jnp.int32, (rc, gc), 0)
            count = jnp.minimum(pos + 1, win).astype(F32)
            pm_ref[r0:r0 + rc, cs] = acc / count - cur
        y = _dot(pm_ref[:, cs].astype(BF16), wg_ref[g]) * sc_ref[:, cs]
        o_ref[:, cs] = y.astype(o_ref.dtype)


def pool_branch(x, prev, w_grp, layer, scale, pos0, out_dtype, tt=256):
    b, t, ch = x.shape
    tt = min(tt, t)
    rc = min(tt, 64)
    top = 16
    seq_spec = pl.BlockSpec((None, tt, ch), lambda i, j: (i, j, 0))
    return pl.pallas_call(
        functools.partial(_pool_branch_kernel, tt=tt, rc=rc, pos0=pos0),
        grid=(b, t // tt),
        in_specs=[seq_spec,
                  pl.BlockSpec((None, POOL_BUF, ch), lambda i, j: (i, 0, 0)),
                  pl.BlockSpec((None,) + w_grp.shape[1:], lambda i, j: (layer, 0, 0, 0)),
                  pl.BlockSpec((1, ch), lambda i, j: (0, 0))],
        out_specs=seq_spec,
        out_shape=jax.ShapeDtypeStruct((b, t, ch), out_dtype),
        scratch_shapes=[pltpu.VMEM((top + tt, ch), F32), pltpu.VMEM((tt, ch), F32)],
        compiler_params=_params("parallel", "arbitrary"),
        name="pool_branch",
    )(x, prev, w_grp, scale.reshape(1, ch))


def _cross_attn_kernel(q_ref, mk_ref, mv_ref, o_ref, *, heads, scale):
    dh = q_ref.shape[1] // heads
    for h in range(heads):
        cs = slice(h * dh, (h + 1) * dh)
        s = _dot_nt(q_ref[:, cs].astype(BF16), mk_ref[:, cs].astype(BF16)) * scale
        e = jnp.exp(s - jnp.max(s, axis=-1, keepdims=True))
        a = e / jnp.sum(e, axis=-1, keepdims=True)
        o_ref[:, cs] = _dot(a.astype(BF16), mv_ref[:, cs].astype(BF16)).astype(o_ref.dtype)


def cross_attention(q, mem_k, mem_v, heads, out_dtype, tq=512):
    b, t, width = q.shape
    n_mem = mem_k.shape[1]
    tq = min(tq, t)
    q_spec = pl.BlockSpec((None, tq, width), lambda i, j: (i, j, 0))
    mem_spec = pl.BlockSpec((None, n_mem, width), lambda i, j: (i, 0, 0))
    return pl.pallas_call(
        functools.partial(_cross_attn_kernel, heads=heads, scale=(width // heads) ** -0.5),
        grid=(b, t // tq),
        in_specs=[q_spec, mem_spec, mem_spec],
        out_specs=q_spec,
        out_shape=jax.ShapeDtypeStruct((b, t, width), out_dtype),
        compiler_params=_params("parallel", "parallel"),
        name="cross_attention",
    )(q, mem_k, mem_v)


def _ffn_up_kernel(x_ref, wv_ref, wg_ref, cwv_ref, cwg_ref, cbv_ref, cbg_ref, *refs,
                   tm, rc, tiles_per_seq, seq_len):
    pad = SUBLANES
    if tiles_per_seq:
        act_ref, tailv_ref, tailg_ref, extv_ref, extg_ref = refs
        fixes = (None, None)
    else:
        f1v_ref, f2v_ref, f1g_ref, f2g_ref, act_ref, tailv_ref, tailg_ref, extv_ref, extg_ref = refs
        fixes = ((f1v_ref, f2v_ref), (f1g_ref, f2g_ref))
    i = pl.program_id(1)
    first = i % max(tiles_per_seq, 1) == 0

    for ext_ref in (extv_ref, extg_ref):
        @pl.when(first)
        def _():
            ext_ref[0:pad, :] = jnp.zeros((pad, ext_ref.shape[1]), F32)

        @pl.when(jnp.logical_not(first))
        def _():
            ext_ref[0:pad, :] = ext_ref[tm:tm + pad, :]

    def conv(u, r0, cw_ref, cb_ref, tail_ref, ext_ref, fix):
        if not tiles_per_seq:
            tail_ref[r0:r0 + rc, :] = u
        elif r0 + rc == tm:
            tail_ref[...] = u[rc - pad:, :]
        ext_ref[pad + r0:pad + r0 + rc, :] = u
        u1 = ext_ref[pad - 1 + r0:pad - 1 + r0 + rc, :]
        u2 = ext_ref[pad - 2 + r0:pad - 2 + r0 + rc, :]
        if fix is not None:
            t_in_seq = (r0 + lax.broadcasted_iota(jnp.int32, u.shape, 0)) % seq_len
            u1 = jnp.where(t_in_seq < 1, fix[0][r0:r0 + rc, :], u1)
            u2 = jnp.where(t_in_seq < 2, fix[1][r0:r0 + rc, :], u2)
        return cb_ref[...] + cw_ref[0:1, :] * u2 + cw_ref[1:2, :] * u1 + cw_ref[2:3, :] * u

    def epilogue(r0, u_val, u_gate):
        val = conv(u_val, r0, cwv_ref, cbv_ref, tailv_ref, extv_ref, fixes[0])
        gate = conv(u_gate, r0, cwg_ref, cbg_ref, tailg_ref, extg_ref, fixes[1])
        act_ref[r0:r0 + rc, :] = (val * (gate * jax.nn.sigmoid(gate))).astype(act_ref.dtype)

    pending = None
    for r0 in range(0, tm, rc):
        x = x_ref[r0:r0 + rc, :].astype(BF16)
        chunk = (r0, _dot(x, wv_ref[...]), _dot(x, wg_ref[...]))
        if pending is not None:
            epilogue(*pending)
        pending = chunk
    epilogue(*pending)


def ffn_up(x, w_up, dw_w, dw_b, seq_len, prev, tm=1024, tn=512):
    m, d = x.shape
    f = w_up.width // 2
    nb = f // tn
    tm = min(tm, m)
    n_tiles = m // tm
    x_spec = pl.BlockSpec((tm, d), lambda j, i: (i, 0))
    col = lambda rows, off: pl.BlockSpec((rows, tn), lambda j, i: (0, j + off))
    tile = lambda off: pl.BlockSpec((tm, tn), lambda j, i: (i, j + off))
    in_specs = [x_spec, w_up.spec(tn, 0), w_up.spec(tn, 0, nb),
                col(FFN_CONV, 0), col(FFN_CONV, nb), col(1, 0), col(1, nb)]
    args = [x, w_up.stack, w_up.stack, dw_w, dw_w, dw_b.reshape(1, 2 * f), dw_b.reshape(1, 2 * f)]
    if prev is None:
        assert seq_len % tm == 0
        tiles_per_seq = seq_len // tm
        tail_rows = SUBLANES
    else:
        assert n_tiles == 1 and seq_len >= FFN_CONV - 1
        tiles_per_seq = 0
        tail_rows = tm
        batch = m // seq_len
        zeros = jnp.zeros((batch, seq_len, 2 * f), F32)
        fix1 = zeros.at[:, 0].set(prev[:, 1]).reshape(m, 2 * f)
        fix2 = zeros.at[:, 0].set(prev[:, 0]).at[:, 1].set(prev[:, 1]).reshape(m, 2 * f)
        in_specs += [tile(0), tile(0), tile(nb), tile(nb)]
        args += [fix1, fix2, fix1, fix2]
    tail_spec = pl.BlockSpec((None, tail_rows, tn), lambda j, i: (i, 0, j))
    tail_shape = jax.ShapeDtypeStruct((n_tiles, tail_rows, f), F32)
    return pl.pallas_call(
        functools.partial(_ffn_up_kernel, tm=tm, rc=min(tm, 128), tiles_per_seq=tiles_per_seq,
                          seq_len=seq_len),
        grid=(nb, n_tiles),
        in_specs=in_specs,
        out_specs=[pl.BlockSpec((tm, tn), lambda j, i: (i, j)), tail_spec, tail_spec],
        out_shape=[jax.ShapeDtypeStruct((m, f), BF16), tail_shape, tail_shape],
        scratch_shapes=[pltpu.VMEM((SUBLANES + tm, tn), F32), pltpu.VMEM((SUBLANES + tm, tn), F32)],
        compiler_params=_params("parallel", "arbitrary"),
        name="ffn_up",
    )(*args)


def _pad_halves(a, half, half_pad):
    widths = [(0, 0)] * (a.ndim - 1) + [(0, half_pad - half)]
    return jnp.concatenate([jnp.pad(a[..., :half], widths), jnp.pad(a[..., half:], widths)], axis=-1)


def _bf16_stacks(p, d_ff_pad):
    d_ff = p['w_down'].shape[1]
    s = {n: p[n].astype(BF16) for n in ('w_in', 'w_sb_out', 'w_cv_out', 'w_pool_out', 'w_mix_out',
                                        'w_xq', 'w_xkv', 'w_xo', 'w_pool_grp')}
    s['w_up'] = _pad_halves(p['w_up'], d_ff, d_ff_pad).astype(BF16)
    s['w_down'] = jnp.pad(p['w_down'], ((0, 0), (0, d_ff_pad - d_ff), (0, 0))).astype(BF16)
    return s


def _layer_weights(l, p, s, d_ff_pad):
    sbw = p['w_sb_out'].shape[1]
    cvc = p['w_cv_out'].shape[1]
    plc = p['w_pool_out'].shape[1]
    offs = [0, sbw, 2 * sbw, 3 * sbw, 3 * sbw + cvc, 3 * sbw + 2 * cvc, 3 * sbw + 2 * cvc + plc]
    names = ['w_q', 'w_k', 'w_v', 'w_cva', 'w_cvb', 'w_pool_in']
    w = {n: weight(s['w_in'], l, a, b - a) for n, a, b in zip(names, offs[:-1], offs[1:])}
    w['w_gates'] = weight(s['w_in'], l, offs[-1])
    for n in ('w_sb_out', 'w_cv_out', 'w_pool_out', 'w_mix_out', 'w_xq', 'w_xo', 'w_up', 'w_down'):
        w[n] = weight(s[n], l)
    xw = p['w_xkv'].shape[2] // 2
    w['w_xk'] = weight(s['w_xkv'], l, 0, xw)
    w['w_xv'] = weight(s['w_xkv'], l, xw, xw)
    w['w_pool_grp'] = s['w_pool_grp']
    d_ff = p['w_down'].shape[1]
    w['ffn_dw_w'] = _pad_halves(p['ffn_dw_w'][l], d_ff, d_ff_pad)
    w['ffn_dw_b'] = _pad_halves(p['ffn_dw_b'][l], d_ff, d_ff_pad)
    for n in ('g_mix', 'sb_bias', 'cv_dw_w', 'cv_dw_b', 'cv_ln_g', 'cv_ln_b', 'pool_scale',
              'g_xattn', 'g_ffn'):
        w[n] = p[n][l]
    return w


def _state(prev, cur, n):
    t = cur.shape[1]
    if t >= n:
        return cur[:, t - n:]
    return jnp.concatenate([prev[:, prev.shape[1] - (n - t):], cur], axis=1)


def _decoder_layer(x, batch, seq, w, mem_k, mem_v, xa_heads, conv_prev, pool_prev, ffn_prev, past, d_ff):
    m, d = x.shape
    heads = w['sb_bias'].shape[0]
    prompt = past is None
    act_dt = BF16 if prompt else F32
    three = lambda a: a.reshape(batch, seq, a.shape[-1])
    two = lambda a: a.reshape(m, a.shape[-1])

    h = rmsnorm(x, w['g_mix'], BF16)
    glu = linear_glu(h, w['w_cva'], w['w_cvb'])
    pool_in, = linear(h, w['w_pool_in'], (F32,))
    if prompt:
        q, = linear(h, w['w_q'], (BF16,))
        k, kb = linear(h, w['w_k'], (F32, BF16))
        v, vb = linear(h, w['w_v'], (F32, BF16))
        attn = sb_attention_prompt(q, kb, vb, w['sb_bias'], batch, seq, heads)
        pos0 = 0
    else:
        q, = linear(h, w['w_q'], (F32,))
        k, = linear(h, w['w_k'], (F32,))
        v, = linear(h, w['w_v'], (F32,))
        cache_k, cache_v, page_table, layer = past
        attn = two(sb_attention_sample(three(q), three(k), three(v), w['sb_bias'],
                                       cache_k, cache_v, page_table, layer, heads))
        pos0 = page_table.shape[1] * (cache_k.shape[2] // heads)
    conv = conv_branch(three(glu), conv_prev, w['cv_dw_w'], w['cv_dw_b'], w['cv_ln_g'], w['cv_ln_b'], act_dt)
    pool = pool_branch(three(pool_in), pool_prev, w['w_pool_grp'], w['w_q'].layer, w['pool_scale'], pos0,
                       act_dt)
    merged = gated_merge(h, attn, two(conv), two(pool), w['w_gates'],
                         w['w_sb_out'], w['w_cv_out'], w['w_pool_out'])
    x = linear_res(merged, w['w_mix_out'], x, tm=1024, tn=512)

    h = rmsnorm(x, w['g_xattn'], BF16)
    qx, = linear(h, w['w_xq'], (act_dt,))
    xa = cross_attention(three(qx), mem_k, mem_v, xa_heads, act_dt)
    x = linear_res(two(xa), w['w_xo'], x, tm=1024, tn=1024)

    h = rmsnorm(x, w['g_ffn'], BF16)
    d_ff_pad = w['w_down'].k
    act, tail_v, tail_g = ffn_up(h, w['w_up'], w['ffn_dw_w'], w['ffn_dw_b'], seq,
                                 None if prompt else _pad_halves(ffn_prev, d_ff, d_ff_pad))
    x = linear_res(act, w['w_down'], x)

    tails = []
    for tail in (tail_v, tail_g):
        if prompt:
            per_seq = tail.shape[0] // batch
            tails.append(tail[per_seq - 1::per_seq, SUBLANES - (FFN_CONV - 1):, :d_ff])
        else:
            tails.append(tail.reshape(batch, seq, -1)[:, seq - (FFN_CONV - 1):, :d_ff])
    ffn_state = jnp.concatenate(tails, axis=-1)
    conv_state = _state(conv_prev, three(glu), CV_WIDTH - 1)
    pool_state = _state(pool_prev, three(pool_in), POOL_BUF)
    return x, k, v, conv_state, pool_state, ffn_state


def kernel(x_prompt, x_sample, mem_prompt, cache_sb_k, cache_sb_v, cache_mem_k, cache_mem_v, state_conv, state_pool, state_ffn, page_table, g_mix, w_in, sb_bias, w_sb_out, cv_dw_w, cv_dw_b, cv_ln_g, cv_ln_b, w_cv_out, w_pool_grp, pool_scale, w_pool_out, w_mix_out, g_xattn, g_mem, w_xq, w_xkv, w_xo, g_ffn, w_up, ffn_dw_w, ffn_dw_b, w_down, g_final):
    p = dict(g_mix=g_mix, w_in=w_in, sb_bias=sb_bias, w_sb_out=w_sb_out, cv_dw_w=cv_dw_w,
             cv_dw_b=cv_dw_b, cv_ln_g=cv_ln_g, cv_ln_b=cv_ln_b, w_cv_out=w_cv_out,
             w_pool_grp=w_pool_grp, pool_scale=pool_scale, w_pool_out=w_pool_out,
             w_mix_out=w_mix_out, g_xattn=g_xattn, w_xq=w_xq, w_xkv=w_xkv, w_xo=w_xo,
             g_ffn=g_ffn, w_up=w_up, ffn_dw_w=ffn_dw_w, ffn_dw_b=ffn_dw_b, w_down=w_down)
    depth = w_in.shape[0]
    bp, tp, d = x_prompt.shape
    bs, ts, _ = x_sample.shape
    heads, dh = cache_sb_k.shape[3], cache_sb_k.shape[4]
    xa_heads, xa_dh = cache_mem_k.shape[3], cache_mem_k.shape[4]
    n_mem = mem_prompt.shape[1]
    d_ff = w_down.shape[1]
    ffn_tn = 512
    d_ff_pad = -(-d_ff // ffn_tn) * ffn_tn
    cache_k = cache_sb_k.reshape(cache_sb_k.shape[:2] + (-1, dh))
    cache_v = cache_sb_v.reshape(cache_sb_v.shape[:2] + (-1, dh))
    mem_flat = mem_prompt.reshape(bp * n_mem, d)

    xp = x_prompt.reshape(bp * tp, d)
    xs = x_sample.reshape(bs * ts, d)
    conv0 = jnp.zeros((bp,) + state_conv.shape[2:], F32)
    pool0 = jnp.zeros((bp,) + state_pool.shape[2:], F32)
    outs_p, outs_s = [], []
    stacks = _bf16_stacks(p, d_ff_pad)
    for l in range(depth):
        w = _layer_weights(l, p, stacks, d_ff_pad)
        mem_h = rmsnorm(mem_flat, g_mem[l], BF16)
        mk, = linear(mem_h, w['w_xk'], (F32,))
        mv, = linear(mem_h, w['w_xv'], (F32,))
        mk3 = mk.reshape(bp, n_mem, xa_heads * xa_dh)
        mv3 = mv.reshape(bp, n_mem, xa_heads * xa_dh)
        xp, kp, vp, cp, pp, fp = _decoder_layer(xp, bp, tp, w, mk3, mv3, xa_heads, conv0, pool0, None,
                                                None, d_ff)
        xs, ks, vs, cs, ps, fs = _decoder_layer(
            xs, bs, ts, w, cache_mem_k[l].reshape(bs, n_mem, -1), cache_mem_v[l].reshape(bs, n_mem, -1),
            xa_heads, state_conv[l], state_pool[l], state_ffn[l], (cache_k, cache_v, page_table, l), d_ff)
        outs_p.append((kp.reshape(bp, tp, heads, dh), vp.reshape(bp, tp, heads, dh),
                       mk.reshape(bp, n_mem, xa_heads, xa_dh), mv.reshape(bp, n_mem, xa_heads, xa_dh),
                       cp, pp, fp))
        outs_s.append((ks.reshape(bs, ts, heads, dh), vs.reshape(bs, ts, heads, dh), cs, ps, fs))
    y_prompt = rmsnorm(xp, g_final, F32).reshape(bp, tp, d)
    y_sample = rmsnorm(xs, g_final, F32).reshape(bs, ts, d)
    stack = lambda rows, i: jnp.stack([r[i] for r in rows])
    return ((y_prompt, y_sample) + tuple(stack(outs_p, i) for i in range(7))
            + tuple(stack(outs_s, i) for i in range(5)))
```

```python
import functools
from typing import NamedTuple

import jax
import jax.numpy as jnp
from jax import lax
from jax.experimental import pallas as pl
from jax.experimental.pallas import tpu as pltpu

F32 = jnp.float32
BF16 = jnp.bfloat16

EPS = 1e-6
CV_WIDTH = 31
POOL_WINDOWS = (2, 4, 8, 16)
POOL_BUF = max(POOL_WINDOWS) - 1
FFN_CONV = 3

VMEM_LIMIT_BYTES = 60 * 2**20
LANES = 128
SUBLANES = 8
MXU_DIM = 256

_NT_DIMS = (((1,), (1,)), ((), ()))


def _params(*semantics):
    return pltpu.CompilerParams(dimension_semantics=semantics, vmem_limit_bytes=VMEM_LIMIT_BYTES)


def _dot(a, b):
    return jnp.dot(a, b, preferred_element_type=F32)


def _dot_nt(a, b):
    return lax.dot_general(a, b, _NT_DIMS, preferred_element_type=F32)


def _rmsnorm_kernel(x_ref, g_ref, o_ref):
    x = x_ref[...]
    ms = jnp.mean(x * x, axis=-1, keepdims=True)
    o_ref[...] = (x * lax.rsqrt(ms + EPS) * g_ref[...]).astype(o_ref.dtype)


def rmsnorm(x, g, out_dtype):
    m, d = x.shape
    tr = min(m, 512)
    return pl.pallas_call(
        _rmsnorm_kernel,
        grid=(m // tr,),
        in_specs=[pl.BlockSpec((tr, d), lambda i: (i, 0)),
                  pl.BlockSpec((1, d), lambda i: (0, 0))],
        out_specs=pl.BlockSpec((tr, d), lambda i: (i, 0)),
        out_shape=jax.ShapeDtypeStruct((m, d), out_dtype),
        compiler_params=_params("parallel"),
        name="rmsnorm",
    )(x, g.reshape(1, d))


def _linear_kernel(x_ref, w_ref, *o_refs):
    acc = _dot(x_ref[...].astype(BF16), w_ref[...])
    for o_ref in o_refs:
        o_ref[...] = acc.astype(o_ref.dtype)


class Weight(NamedTuple):
    stack: jax.Array
    layer: int
    col0: int
    width: int

    @property
    def k(self):
        return self.stack.shape[1]

    def spec(self, tn, col_axis, block_off=0):
        assert self.col0 % tn == 0 and self.width % tn == 0
        first = self.col0 // tn + block_off
        layer = self.layer
        return pl.BlockSpec((None, self.k, tn), lambda *g: (layer, 0, first + g[col_axis]))


def weight(stack, layer, col0=0, width=None):
    return Weight(stack, layer, col0, stack.shape[2] - col0 if width is None else width)


def linear(x, w, out_dtypes, tm=1024, tn=1024):
    m, k = x.shape
    n = w.width
    tm, tn = min(tm, m), min(tn, n)
    outs = pl.pallas_call(
        _linear_kernel,
        grid=(n // tn, m // tm),
        in_specs=[pl.BlockSpec((tm, k), lambda j, i: (i, 0)), w.spec(tn, 0)],
        out_specs=[pl.BlockSpec((tm, tn), lambda j, i: (i, j)) for _ in out_dtypes],
        out_shape=[jax.ShapeDtypeStruct((m, n), dt) for dt in out_dtypes],
        compiler_params=_params("parallel", "parallel"),
        name="linear",
    )(x, w.stack)
    return outs


def _linear_glu_kernel(x_ref, wa_ref, wb_ref, o_ref):
    x = x_ref[...].astype(BF16)
    o_ref[...] = _dot(x, wa_ref[...]) * jax.nn.sigmoid(_dot(x, wb_ref[...]))


def linear_glu(x, wa, wb, tm=1024, tn=512):
    m, k = x.shape
    n = wa.width
    tm, tn = min(tm, m), min(tn, n)
    return pl.pallas_call(
        _linear_glu_kernel,
        grid=(n // tn, m // tm),
        in_specs=[pl.BlockSpec((tm, k), lambda j, i: (i, 0)), wa.spec(tn, 0), wb.spec(tn, 0)],
        out_specs=pl.BlockSpec((tm, tn), lambda j, i: (i, j)),
        out_shape=jax.ShapeDtypeStruct((m, n), F32),
        compiler_params=_params("parallel", "parallel"),
        name="linear_glu",
    )(x, wa.stack, wb.stack)


def _linear_res_kernel(x_ref, w_ref, r_ref, o_ref):
    o_ref[...] = r_ref[...] + _dot(x_ref[...].astype(BF16), w_ref[...])


def linear_res(x, w, res, tm=512, tn=512):
    m, k = x.shape
    n = w.width
    tm, tn = min(tm, m), min(tn, n)
    return pl.pallas_call(
        _linear_res_kernel,
        grid=(n // tn, m // tm),
        in_specs=[pl.BlockSpec((tm, k), lambda j, i: (i, 0)), w.spec(tn, 0),
                  pl.BlockSpec((tm, tn), lambda j, i: (i, j))],
        out_specs=pl.BlockSpec((tm, tn), lambda j, i: (i, j)),
        out_shape=jax.ShapeDtypeStruct((m, n), F32),
        compiler_params=_params("parallel", "parallel"),
        name="linear_res",
    )(x, w.stack, res)


def _merge_kernel(h_ref, a_ref, c_ref, p_ref, wg0_ref, wg1_ref, wg2_ref,
                  wsb_ref, wcv_ref, wpl_ref, o_ref):
    h = h_ref[...].astype(BF16)

    def gated(wg_ref, y_ref, w_ref):
        gate = jax.nn.sigmoid(_dot(h, wg_ref[...]))
        return gate * _dot(y_ref[...].astype(BF16), w_ref[...])

    merged = (gated(wg0_ref, a_ref, wsb_ref) + gated(wg1_ref, c_ref, wcv_ref)
              + gated(wg2_ref, p_ref, wpl_ref))
    o_ref[...] = merged.astype(o_ref.dtype)


def gated_merge(h, attn, conv, pool, w_gates, w_sb_out, w_cv_out, w_pool_out, tm=512, tn=256):
    m, d = h.shape
    n = w_sb_out.width
    tm = min(tm, m)
    nb = n // tn

    def rows(a):
        return pl.BlockSpec((tm, a.shape[1]), lambda i, j: (i, 0))

    return pl.pallas_call(
        _merge_kernel,
        grid=(m // tm, nb),
        in_specs=[rows(h), rows(attn), rows(conv), rows(pool),
                  w_gates.spec(tn, 1), w_gates.spec(tn, 1, nb), w_gates.spec(tn, 1, 2 * nb),
                  w_sb_out.spec(tn, 1), w_cv_out.spec(tn, 1), w_pool_out.spec(tn, 1)],
        out_specs=pl.BlockSpec((tm, tn), lambda i, j: (i, j)),
        out_shape=jax.ShapeDtypeStruct((m, n), BF16),
        compiler_params=_params("parallel", "parallel"),
        name="gated_merge",
    )(h, attn, conv, pool, w_gates.stack, w_gates.stack, w_gates.stack,
      w_sb_out.stack, w_cv_out.stack, w_pool_out.stack)


def _fail_nats(z):
    return jnp.maximum(z, 0.0) + jnp.log(1.0 + jnp.exp(-jnp.abs(z)))


def _suffix_sum(f, u2):
    f_hi = f.astype(BF16)
    f_lo = (f - f_hi.astype(F32)).astype(BF16)
    return _dot(jnp.concatenate([f_hi, f_lo], axis=1), u2)


def _sb_blocks(zs, visible, u2):
    fs = [_fail_nats(z) for z in zs]
    log_hits = [z - f for z, f in zip(zs, fs)]
    if visible is not None:
        fs = [jnp.where(visible, f, 0.0) for f in fs]
    sufs = [_suffix_sum(f, u2) for f in fs]
    return ([lh - s for lh, s in zip(log_hits, sufs)],
            [jnp.sum(f, axis=-1, keepdims=True) for f in fs])


def _sb_weights(log_w, r, visible):
    w = jnp.exp(log_w - r)
    if visible is not None:
        w = jnp.where(visible, w, 0.0)
    return w.astype(BF16)


def _sb_prompt_kernel(q_ref, k_ref, v_ref, bias_ref, u_ref, o_ref, r_ref, acc_ref, *, tq, dh, scale):
    i = pl.program_id(2)
    u = u_ref[...]
    r_ref[...] = jnp.zeros_like(r_ref)
    acc_ref[...] = jnp.zeros_like(acc_ref)

    def block(kb, visible):
        start = pl.multiple_of(kb * tq, tq)
        cols = [slice(hh * dh, (hh + 1) * dh) for hh in range(q_ref.shape[1] // dh)]
        zs = [_dot_nt(q_ref[:, cs], k_ref[pl.ds(start, tq), cs]) * scale + bias_ref[hh:hh + 1, :]
              for hh, cs in enumerate(cols)]
        log_ws, fsums = _sb_blocks(zs, visible, u)
        for hh, cs in enumerate(cols):
            w = _sb_weights(log_ws[hh], r_ref[hh], visible)
            acc_ref[:, cs] += _dot(w, v_ref[pl.ds(start, tq), cs])
            r_ref[hh] += fsums[hh]

    row = lax.broadcasted_iota(jnp.int32, (tq, tq), 0)
    col = lax.broadcasted_iota(jnp.int32, (tq, tq), 1)
    block(i, col < row)

    def older(n, carry):
        block(i - 1 - n, None)
        return carry

    lax.fori_loop(0, i, older, 0)
    o_ref[...] = acc_ref[...].astype(o_ref.dtype)


def _suffix_matrix(n):
    j = lax.broadcasted_iota(jnp.int32, (2 * n, n), 0) % n
    s = lax.broadcasted_iota(jnp.int32, (2 * n, n), 1)
    return (j > s).astype(BF16)


def sb_attention_prompt(q, k, v, bias, batch, seq, heads, tq=256, heads_per_step=8):
    m, width = q.shape
    dh = width // heads
    nq = seq // tq
    hps = heads_per_step
    bias_b = jnp.broadcast_to(bias.astype(F32).reshape(heads // hps, hps, 1), (heads // hps, hps, tq))
    kv_spec = pl.BlockSpec((seq, hps * dh), lambda b, h, i: (b, h))
    q_spec = pl.BlockSpec((tq, hps * dh), lambda b, h, i: (b * nq + i, h))
    return pl.pallas_call(
        functools.partial(_sb_prompt_kernel, tq=tq, dh=dh, scale=dh ** -0.5),
        grid=(batch, heads // hps, nq),
        in_specs=[q_spec, kv_spec, kv_spec,
                  pl.BlockSpec((None, hps, tq), lambda b, h, i: (h, 0, 0)),
                  pl.BlockSpec((2 * tq, tq), lambda b, h, i: (0, 0))],
        out_specs=q_spec,
        out_shape=jax.ShapeDtypeStruct((m, width), BF16),
        scratch_shapes=[pltpu.VMEM((hps, tq, 1), F32), pltpu.VMEM((tq, hps * dh), F32)],
        compiler_params=_params("parallel", "parallel", "parallel"),
        name="sb_attention_prompt",
    )(q, k, v, bias_b, _suffix_matrix(tq))


def _sb_sample_kernel(pt_ref, q_ref, kn_ref, vn_ref, bias_ref, u_ref, *refs,
                      pages_per_step, n_q, dh, scale):
    del pt_ref
    k_refs = refs[:pages_per_step]
    v_refs = refs[pages_per_step:2 * pages_per_step]
    o_ref, r_ref, acc_ref, qbd_ref = refs[2 * pages_per_step:]
    step = pl.program_id(1)
    width = q_ref.shape[1]
    heads = width // dh
    page = k_refs[0].shape[0] // heads
    cw = 2 * dh
    cr = 2 * n_q
    n_chunk = width // cw
    lane = lax.broadcasted_iota(jnp.int32, (n_q, cw), 1)

    def cached(ref):
        def chunk(c):
            return jnp.concatenate([ref[pl.ds(2 * c, page, stride=heads), :],
                                    ref[pl.ds(2 * c + 1, page, stride=heads), :]], axis=1).astype(BF16)
        return chunk

    def process(pages, visible):
        zs = [jnp.concatenate(
            [_dot_nt(qbd_ref[c * cr:(c + 1) * cr, :], k_chunk(c)) for c in range(n_chunk)],
            axis=0) * scale + bias_ref[...] for k_chunk, _ in pages]
        log_ws, fsums = _sb_blocks(zs, visible, u_ref[...])
        r = r_ref[...]
        for log_w, fsum, (_, v_chunk) in zip(log_ws, fsums, pages):
            w = _sb_weights(log_w, r, visible)
            for c in range(n_chunk):
                acc_ref[c * cr:(c + 1) * cr, :] += _dot(w[c * cr:(c + 1) * cr, :], v_chunk(c))
            r = r + fsum
        r_ref[...] = r

    @pl.when(step == 0)
    def _():
        r_ref[...] = jnp.zeros_like(r_ref)
        acc_ref[...] = jnp.zeros_like(acc_ref)
        q = q_ref[...]
        for c in range(n_chunk):
            qc = q[:, c * cw:(c + 1) * cw]
            qbd_ref[c * cr:(c + 1) * cr, :] = jnp.concatenate(
                [jnp.where(lane < dh, qc, 0.0), jnp.where(lane >= dh, qc, 0.0)], axis=0).astype(BF16)
        pad = jnp.zeros((page - n_q, cw), F32)
        row = lax.broadcasted_iota(jnp.int32, (n_chunk * cr, page), 0)
        col = lax.broadcasted_iota(jnp.int32, (n_chunk * cr, page), 1)

        def fresh(ref):
            def chunk(c):
                return jnp.concatenate([ref[:, c * cw:(c + 1) * cw], pad], axis=0).astype(BF16)
            return chunk

        process([(fresh(kn_ref), fresh(vn_ref))], col < row % n_q)

    process([(cached(k_ref), cached(v_ref)) for k_ref, v_ref in zip(k_refs, v_refs)], None)

    @pl.when(step == pl.num_programs(1) - 1)
    def _():
        for c in range(n_chunk):
            o_ref[:, c * cw:(c + 1) * cw] = jnp.where(
                lane < dh, acc_ref[c * cr:c * cr + n_q, :], acc_ref[c * cr + n_q:(c + 1) * cr, :])


def sb_attention_sample(q, k_new, v_new, bias, cache_k, cache_v, page_table, layer, heads,
                        pages_per_step=8):
    batch, n_q, width = q.shape
    dh = width // heads
    page = cache_k.shape[2] // heads
    n_pages = page_table.shape[1]
    assert n_q == SUBLANES and page == LANES and n_pages % pages_per_step == 0
    rows = heads * n_q
    bias_rows = jnp.repeat(bias.astype(F32), n_q).reshape(rows, 1)

    def new_spec():
        return pl.BlockSpec((None, n_q, width), lambda b, s, pt: (b, 0, 0))

    def page_spec(slot):
        def index_map(b, s, pt):
            return (layer, pt[b, n_pages - 1 - (s * pages_per_step + slot)], 0, 0)
        return pl.BlockSpec((None, None, page * heads, dh), index_map)

    page_specs = [page_spec(i) for i in range(pages_per_step)]
    grid_spec = pltpu.PrefetchScalarGridSpec(
        num_scalar_prefetch=1,
        grid=(batch, n_pages // pages_per_step),
        in_specs=[new_spec(), new_spec(), new_spec(),
                  pl.BlockSpec((rows, 1), lambda b, s, pt: (0, 0)),
                  pl.BlockSpec((2 * page, page), lambda b, s, pt: (0, 0))] + page_specs + page_specs,
        out_specs=new_spec(),
        scratch_shapes=[pltpu.VMEM((rows, 1), F32), pltpu.VMEM((rows, 2 * dh), F32),
                        pltpu.VMEM((rows, 2 * dh), BF16)],
    )
    return pl.pallas_call(
        functools.partial(_sb_sample_kernel, pages_per_step=pages_per_step, n_q=n_q, dh=dh,
                          scale=dh ** -0.5),
        grid_spec=grid_spec,
        out_shape=jax.ShapeDtypeStruct((batch, n_q, width), F32),
        compiler_params=_params("parallel", "arbitrary"),
        name="sb_attention_sample",
    )(page_table, q, k_new, v_new, bias_rows, _suffix_matrix(page),
      *([cache_k] * pages_per_step), *([cache_v] * pages_per_step))


def _conv_branch_kernel(x_ref, prev_ref, w_ref, b_ref, g_ref, beta_ref, o_ref, ext_ref, sh_ref, y_ref,
                        *, tt, rc, cc):
    t = pl.program_id(1)
    halo = CV_WIDTH - 1
    top = ext_ref.shape[0] - tt
    ch = x_ref.shape[1]

    @pl.when(t == 0)
    def _():
        ext_ref[top - halo:top, :] = prev_ref[...]

    @pl.when(t > 0)
    def _():
        ext_ref[0:top, :] = ext_ref[tt:tt + top, :]

    ext_ref[top:top + tt, :] = x_ref[...]
    n_ext = top + tt
    for s in range(1, SUBLANES):
        sh_ref[s, 0:n_ext - SUBLANES, :] = ext_ref[s:s + n_ext - SUBLANES, :]

    def rows(lo):
        s = lo % SUBLANES
        if s == 0:
            return ext_ref, lo
        return sh_ref.at[s], lo - s

    for c0 in range(0, ch, cc):
        cs = slice(c0, c0 + cc)
        for r0 in range(0, tt, rc):
            acc = jnp.broadcast_to(b_ref[:, cs], (rc, cc))
            for j in range(CV_WIDTH):
                src, lo = rows(top - halo + r0 + j)
                acc = acc + src[lo:lo + rc, cs] * w_ref[j:j + 1, cs]
            y_ref[r0:r0 + rc, cs] = acc

    nr = min(tt, 32)
    for r0 in range(0, tt, nr):
        y = y_ref[r0:r0 + nr, :]
        yc = y - jnp.mean(y, axis=-1, keepdims=True)
        var = jnp.mean(yc * yc, axis=-1, keepdims=True)
        z = yc * lax.rsqrt(var + EPS) * g_ref[...] + beta_ref[...]
        o_ref[r0:r0 + nr, :] = (z * jax.nn.sigmoid(z)).astype(o_ref.dtype)


def conv_branch(glu, prev, dw_w, dw_b, ln_g, ln_b, out_dtype, tt=256):
    b, t, ch = glu.shape
    tt = min(tt, t)
    rc = min(tt, 64)
    top = 32
    seq_spec = pl.BlockSpec((None, tt, ch), lambda i, j: (i, j, 0))
    vec_spec = pl.BlockSpec((1, ch), lambda i, j: (0, 0))
    return pl.pallas_call(
        functools.partial(_conv_branch_kernel, tt=tt, rc=rc, cc=256),
        grid=(b, t // tt),
        in_specs=[seq_spec,
                  pl.BlockSpec((None, CV_WIDTH - 1, ch), lambda i, j: (i, 0, 0)),
                  pl.BlockSpec((CV_WIDTH, ch), lambda i, j: (0, 0)),
                  vec_spec, vec_spec, vec_spec],
        out_specs=seq_spec,
        out_shape=jax.ShapeDtypeStruct((b, t, ch), out_dtype),
        scratch_shapes=[pltpu.VMEM((top + tt, ch), F32), pltpu.VMEM((SUBLANES, top + tt, ch), F32),
                        pltpu.VMEM((tt, ch), F32)],
        compiler_params=_params("parallel", "arbitrary"),
        name="conv_branch",
    )(glu, prev, dw_w, dw_b.reshape(1, ch), ln_g.reshape(1, ch), ln_b.reshape(1, ch))


def _pool_branch_kernel(x_ref, prev_ref, wg_ref, sc_ref, o_ref, ext_ref, pm_ref, *, tt, rc, pos0):
    t = pl.program_id(1)
    top = ext_ref.shape[0] - tt
    gc = wg_ref.shape[1]

    @pl.when(t == 0)
    def _():
        ext_ref[top - POOL_BUF:top, :] = prev_ref[...]

    @pl.when(t > 0)
    def _():
        ext_ref[0:top, :] = ext_ref[tt:tt + top, :]

    ext_ref[top:top + tt, :] = x_ref[...]

    for g, win in enumerate(POOL_WINDOWS):
        cs = slice(g * gc, (g + 1) * gc)
        for r0 in range(0, tt, rc):
            cur = ext_ref[top + r0:top + r0 + rc, cs]
            acc = cur
            for i in range(1, win):
                acc = acc + ext_ref[top + r0 - i:top + r0 - i + rc, cs]
            pos = pos0 + t * tt + r0 + lax.broadcasted_iota(jnp.int32, (rc, gc), 0)
            count = jnp.minimum(pos + 1, win).astype(F32)
            pm_ref[r0:r0 + rc, cs] = acc / count - cur
        y = _dot(pm_ref[:, cs].astype(BF16), wg_ref[g]) * sc_ref[:, cs]
        o_ref[:, cs] = y.astype(o_ref.dtype)


def pool_branch(x, prev, w_grp, layer, scale, pos0, out_dtype, tt=256):
    b, t, ch = x.shape
    tt = min(tt, t)
    rc = min(tt, 64)
    top = 16
    seq_spec = pl.BlockSpec((None, tt, ch), lambda i, j: (i, j, 0))
    return pl.pallas_call(
        functools.partial(_pool_branch_kernel, tt=tt, rc=rc, pos0=pos0),
        grid=(b, t // tt),
        in_specs=[seq_spec,
                  pl.BlockSpec((None, POOL_BUF, ch), lambda i, j: (i, 0, 0)),
                  pl.BlockSpec((None,) + w_grp.shape[1:], lambda i, j: (layer, 0, 0, 0)),
                  pl.BlockSpec((1, ch), lambda i, j: (0, 0))],
        out_specs=seq_spec,
        out_shape=jax.ShapeDtypeStruct((b, t, ch), out_dtype),
        scratch_shapes=[pltpu.VMEM((top + tt, ch), F32), pltpu.VMEM((tt, ch), F32)],
        compiler_params=_params("parallel", "arbitrary"),
        name="pool_branch",
    )(x, prev, w_grp, scale.reshape(1, ch))


def _cross_attn_kernel(q_ref, mk_ref, mv_ref, o_ref, *, heads, scale):
    dh = q_ref.shape[1] // heads
    for h in range(heads):
        cs = slice(h * dh, (h + 1) * dh)
        s = _dot_nt(q_ref[:, cs].astype(BF16), mk_ref[:, cs].astype(BF16)) * scale
        e = jnp.exp(s - jnp.max(s, axis=-1, keepdims=True))
        a = e / jnp.sum(e, axis=-1, keepdims=True)
        o_ref[:, cs] = _dot(a.astype(BF16), mv_ref[:, cs].astype(BF16)).astype(o_ref.dtype)


def cross_attention(q, mem_k, mem_v, heads, out_dtype, tq=512):
    b, t, width = q.shape
    n_mem = mem_k.shape[1]
    tq = min(tq, t)
    q_spec = pl.BlockSpec((None, tq, width), lambda i, j: (i, j, 0))
    mem_spec = pl.BlockSpec((None, n_mem, width), lambda i, j: (i, 0, 0))
    return pl.pallas_call(
        functools.partial(_cross_attn_kernel, heads=heads, scale=(width // heads) ** -0.5),
        grid=(b, t // tq),
        in_specs=[q_spec, mem_spec, mem_spec],
        out_specs=q_spec,
        out_shape=jax.ShapeDtypeStruct((b, t, width), out_dtype),
        compiler_params=_params("parallel", "parallel"),
        name="cross_attention",
    )(q, mem_k, mem_v)


def _ffn_up_kernel(x_ref, wv_ref, wg_ref, cwv_ref, cwg_ref, cbv_ref, cbg_ref, *refs,
                   tm, rc, tiles_per_seq, seq_len):
    pad = SUBLANES
    if tiles_per_seq:
        act_ref, tailv_ref, tailg_ref, extv_ref, extg_ref = refs
        fixes = (None, None)
    else:
        f1v_ref, f2v_ref, f1g_ref, f2g_ref, act_ref, tailv_ref, tailg_ref, extv_ref, extg_ref = refs
        fixes = ((f1v_ref, f2v_ref), (f1g_ref, f2g_ref))
    i = pl.program_id(1)
    first = i % max(tiles_per_seq, 1) == 0

    for ext_ref in (extv_ref, extg_ref):
        @pl.when(first)
        def _():
            ext_ref[0:pad, :] = jnp.zeros((pad, ext_ref.shape[1]), F32)

        @pl.when(jnp.logical_not(first))
        def _():
            ext_ref[0:pad, :] = ext_ref[tm:tm + pad, :]

    def conv(u, r0, cw_ref, cb_ref, tail_ref, ext_ref, fix):
        if not tiles_per_seq:
            tail_ref[r0:r0 + rc, :] = u
        elif r0 + rc == tm:
            tail_ref[...] = u[rc - pad:, :]
        ext_ref[pad + r0:pad + r0 + rc, :] = u
        u1 = ext_ref[pad - 1 + r0:pad - 1 + r0 + rc, :]
        u2 = ext_ref[pad - 2 + r0:pad - 2 + r0 + rc, :]
        if fix is not None:
            t_in_seq = (r0 + lax.broadcasted_iota(jnp.int32, u.shape, 0)) % seq_len
            u1 = jnp.where(t_in_seq < 1, fix[0][r0:r0 + rc, :], u1)
            u2 = jnp.where(t_in_seq < 2, fix[1][r0:r0 + rc, :], u2)
        return cb_ref[...] + cw_ref[0:1, :] * u2 + cw_ref[1:2, :] * u1 + cw_ref[2:3, :] * u

    def epilogue(r0, u_val, u_gate):
        val = conv(u_val, r0, cwv_ref, cbv_ref, tailv_ref, extv_ref, fixes[0])
        gate = conv(u_gate, r0, cwg_ref, cbg_ref, tailg_ref, extg_ref, fixes[1])
        act_ref[r0:r0 + rc, :] = (val * (gate * jax.nn.sigmoid(gate))).astype(act_ref.dtype)

    pending = None
    for r0 in range(0, tm, rc):
        x = x_ref[r0:r0 + rc, :].astype(BF16)
        chunk = (r0, _dot(x, wv_ref[...]), _dot(x, wg_ref[...]))
        if pending is not None:
            epilogue(*pending)
        pending = chunk
    epilogue(*pending)


def ffn_up(x, w_up, dw_w, dw_b, seq_len, prev, tm=1024, tn=512):
    m, d = x.shape
    f = w_up.width // 2
    nb = f // tn
    tm = min(tm, m)
    n_tiles = m // tm
    x_spec = pl.BlockSpec((tm, d), lambda j, i: (i, 0))
    col = lambda rows, off: pl.BlockSpec((rows, tn), lambda j, i: (0, j + off))
    tile = lambda off: pl.BlockSpec((tm, tn), lambda j, i: (i, j + off))
    in_specs = [x_spec, w_up.spec(tn, 0), w_up.spec(tn, 0, nb),
                col(FFN_CONV, 0), col(FFN_CONV, nb), col(1, 0), col(1, nb)]
    args = [x, w_up.stack, w_up.stack, dw_w, dw_w, dw_b.reshape(1, 2 * f), dw_b.reshape(1, 2 * f)]
    if prev is None:
        assert seq_len % tm == 0
        tiles_per_seq = seq_len // tm
        tail_rows = SUBLANES
    else:
        assert n_tiles == 1 and seq_len >= FFN_CONV - 1
        tiles_per_seq = 0
        tail_rows = tm
        batch = m // seq_len
        zeros = jnp.zeros((batch, seq_len, 2 * f), F32)
        fix1 = zeros.at[:, 0].set(prev[:, 1]).reshape(m, 2 * f)
        fix2 = zeros.at[:, 0].set(prev[:, 0]).at[:, 1].set(prev[:, 1]).reshape(m, 2 * f)
        in_specs += [tile(0), tile(0), tile(nb), tile(nb)]
        args += [fix1, fix2, fix1, fix2]
    tail_spec = pl.BlockSpec((None, tail_rows, tn), lambda j, i: (i, 0, j))
    tail_shape = jax.ShapeDtypeStruct((n_tiles, tail_rows, f), F32)
    return pl.pallas_call(
        functools.partial(_ffn_up_kernel, tm=tm, rc=min(tm, 128), tiles_per_seq=tiles_per_seq,
                          seq_len=seq_len),
        grid=(nb, n_tiles),
        in_specs=in_specs,
        out_specs=[pl.BlockSpec((tm, tn), lambda j, i: (i, j)), tail_spec, tail_spec],
        out_shape=[jax.ShapeDtypeStruct((m, f), BF16), tail_shape, tail_shape],
        scratch_shapes=[pltpu.VMEM((SUBLANES + tm, tn), F32), pltpu.VMEM((SUBLANES + tm, tn), F32)],
        compiler_params=_params("parallel", "arbitrary"),
        name="ffn_up",
    )(*args)


def _pad_halves(a, half, half_pad):
    widths = [(0, 0)] * (a.ndim - 1) + [(0, half_pad - half)]
    return jnp.concatenate([jnp.pad(a[..., :half], widths), jnp.pad(a[..., half:], widths)], axis=-1)


def _bf16_stacks(p, d_ff_pad):
    d_ff = p['w_down'].shape[1]
    s = {n: p[n].astype(BF16) for n in ('w_in', 'w_sb_out', 'w_cv_out', 'w_pool_out', 'w_mix_out',
                                        'w_xq', 'w_xkv', 'w_xo', 'w_pool_grp')}
    s['w_up'] = _pad_halves(p['w_up'], d_ff, d_ff_pad).astype(BF16)
    s['w_down'] = jnp.pad(p['w_down'], ((0, 0), (0, d_ff_pad - d_ff), (0, 0))).astype(BF16)
    return s


def _layer_weights(l, p, s, d_ff_pad):
    sbw = p['w_sb_out'].shape[1]
    cvc = p['w_cv_out'].shape[1]
    plc = p['w_pool_out'].shape[1]
    offs = [0, sbw, 2 * sbw, 3 * sbw, 3 * sbw + cvc, 3 * sbw + 2 * cvc, 3 * sbw + 2 * cvc + plc]
    names = ['w_q', 'w_k', 'w_v', 'w_cva', 'w_cvb', 'w_pool_in']
    w = {n: weight(s['w_in'], l, a, b - a) for n, a, b in zip(names, offs[:-1], offs[1:])}
    w['w_gates'] = weight(s['w_in'], l, offs[-1])
    for n in ('w_sb_out', 'w_cv_out', 'w_pool_out', 'w_mix_out', 'w_xq', 'w_xo', 'w_up', 'w_down'):
        w[n] = weight(s[n], l)
    xw = p['w_xkv'].shape[2] // 2
    w['w_xk'] = weight(s['w_xkv'], l, 0, xw)
    w['w_xv'] = weight(s['w_xkv'], l, xw, xw)
    w['w_pool_grp'] = s['w_pool_grp']
    d_ff = p['w_down'].shape[1]
    w['ffn_dw_w'] = _pad_halves(p['ffn_dw_w'][l], d_ff, d_ff_pad)
    w['ffn_dw_b'] = _pad_halves(p['ffn_dw_b'][l], d_ff, d_ff_pad)
    for n in ('g_mix', 'sb_bias', 'cv_dw_w', 'cv_dw_b', 'cv_ln_g', 'cv_ln_b', 'pool_scale',
              'g_xattn', 'g_ffn'):
        w[n] = p[n][l]
    return w


def _state(prev, cur, n):
    t = cur.shape[1]
    if t >= n:
        return cur[:, t - n:]
    return jnp.concatenate([prev[:, prev.shape[1] - (n - t):], cur], axis=1)


def _decoder_layer(x, batch, seq, w, mem_k, mem_v, xa_heads, conv_prev, pool_prev, ffn_prev, past, d_ff):
    m, d = x.shape
    heads = w['sb_bias'].shape[0]
    prompt = past is None
    act_dt = BF16 if prompt else F32
    three = lambda a: a.reshape(batch, seq, a.shape[-1])
    two = lambda a: a.reshape(m, a.shape[-1])

    h = rmsnorm(x, w['g_mix'], BF16)
    glu = linear_glu(h, w['w_cva'], w['w_cvb'])
    pool_in, = linear(h, w['w_pool_in'], (F32,))
    if prompt:
        q, = linear(h, w['w_q'], (BF16,))
        k, kb = linear(h, w['w_k'], (F32, BF16))
        v, vb = linear(h, w['w_v'], (F32, BF16))
        attn = sb_attention_prompt(q, kb, vb, w['sb_bias'], batch, seq, heads)
        pos0 = 0
    else:
        q, = linear(h, w['w_q'], (F32,))
        k, = linear(h, w['w_k'], (F32,))
        v, = linear(h, w['w_v'], (F32,))
        cache_k, cache_v, page_table, layer = past
        attn = two(sb_attention_sample(three(q), three(k), three(v), w['sb_bias'],
                                       cache_k, cache_v, page_table, layer, heads))
        pos0 = page_table.shape[1] * (cache_k.shape[2] // heads)
    conv = conv_branch(three(glu), conv_prev, w['cv_dw_w'], w['cv_dw_b'], w['cv_ln_g'], w['cv_ln_b'], act_dt)
    pool = pool_branch(three(pool_in), pool_prev, w['w_pool_grp'], w['w_q'].layer, w['pool_scale'], pos0,
                       act_dt)
    merged = gated_merge(h, attn, two(conv), two(pool), w['w_gates'],
                         w['w_sb_out'], w['w_cv_out'], w['w_pool_out'])
    x = linear_res(merged, w['w_mix_out'], x, tm=1024, tn=512)

    h = rmsnorm(x, w['g_xattn'], BF16)
    qx, = linear(h, w['w_xq'], (act_dt,))
    xa = cross_attention(three(qx), mem_k, mem_v, xa_heads, act_dt)
    x = linear_res(two(xa), w['w_xo'], x, tm=1024, tn=1024)

    h = rmsnorm(x, w['g_ffn'], BF16)
    d_ff_pad = w['w_down'].k
    act, tail_v, tail_g = ffn_up(h, w['w_up'], w['ffn_dw_w'], w['ffn_dw_b'], seq,
                                 None if prompt else _pad_halves(ffn_prev, d_ff, d_ff_pad))
    x = linear_res(act, w['w_down'], x)

    tails = []
    for tail in (tail_v, tail_g):
        if prompt:
            per_seq = tail.shape[0] // batch
            tails.append(tail[per_seq - 1::per_seq, SUBLANES - (FFN_CONV - 1):, :d_ff])
        else:
            tails.append(tail.reshape(batch, seq, -1)[:, seq - (FFN_CONV - 1):, :d_ff])
    ffn_state = jnp.concatenate(tails, axis=-1)
    conv_state = _state(conv_prev, three(glu), CV_WIDTH - 1)
    pool_state = _state(pool_prev, three(pool_in), POOL_BUF)
    return x, k, v, conv_state, pool_state, ffn_state


def kernel(x_prompt, x_sample, mem_prompt, cache_sb_k, cache_sb_v, cache_mem_k, cache_mem_v, state_conv, state_pool, state_ffn, page_table, g_mix, w_in, sb_bias, w_sb_out, cv_dw_w, cv_dw_b, cv_ln_g, cv_ln_b, w_cv_out, w_pool_grp, pool_scale, w_pool_out, w_mix_out, g_xattn, g_mem, w_xq, w_xkv, w_xo, g_ffn, w_up, ffn_dw_w, ffn_dw_b, w_down, g_final):
    p = dict(g_mix=g_mix, w_in=w_in, sb_bias=sb_bias, w_sb_out=w_sb_out, cv_dw_w=cv_dw_w,
             cv_dw_b=cv_dw_b, cv_ln_g=cv_ln_g, cv_ln_b=cv_ln_b, w_cv_out=w_cv_out,
             w_pool_grp=w_pool_grp, pool_scale=pool_scale, w_pool_out=w_pool_out,
             w_mix_out=w_mix_out, g_xattn=g_xattn, w_xq=w_xq, w_xkv=w_xkv, w_xo=w_xo,
             g_ffn=g_ffn, w_up=w_up, ffn_dw_w=ffn_dw_w, ffn_dw_b=ffn_dw_b, w_down=w_down)
    depth = w_in.shape[0]
    bp, tp, d = x_prompt.shape
    bs, ts, _ = x_sample.shape
    heads, dh = cache_sb_k.shape[3], cache_sb_k.shape[4]
    xa_heads, xa_dh = cache_mem_k.shape[3], cache_mem_k.shape[4]
    n_mem = mem_prompt.shape[1]
    d_ff = w_down.shape[1]
    ffn_tn = 512
    d_ff_pad = -(-d_ff // ffn_tn) * ffn_tn
    cache_k = cache_sb_k.reshape(cache_sb_k.shape[:2] + (-1, dh))
    cache_v = cache_sb_v.reshape(cache_sb_v.shape[:2] + (-1, dh))
    mem_flat = mem_prompt.reshape(bp * n_mem, d)

    xp = x_prompt.reshape(bp * tp, d)
    xs = x_sample.reshape(bs * ts, d)
    conv0 = jnp.zeros((bp,) + state_conv.shape[2:], F32)
    pool0 = jnp.zeros((bp,) + state_pool.shape[2:], F32)
    outs_p, outs_s = [], []
    stacks = _bf16_stacks(p, d_ff_pad)
    for l in range(depth):
        w = _layer_weights(l, p, stacks, d_ff_pad)
        mem_h = rmsnorm(mem_flat, g_mem[l], BF16)
        mk, = linear(mem_h, w['w_xk'], (F32,))
        mv, = linear(mem_h, w['w_xv'], (F32,))
        mk3 = mk.reshape(bp, n_mem, xa_heads * xa_dh)
        mv3 = mv.reshape(bp, n_mem, xa_heads * xa_dh)
        xp, kp, vp, cp, pp, fp = _decoder_layer(xp, bp, tp, w, mk3, mv3, xa_heads, conv0, pool0, None,
                                                None, d_ff)
        xs, ks, vs, cs, ps, fs = _decoder_layer(
            xs, bs, ts, w, cache_mem_k[l].reshape(bs, n_mem, -1), cache_mem_v[l].reshape(bs, n_mem, -1),
            xa_heads, state_conv[l], state_pool[l], state_ffn[l], (cache_k, cache_v, page_table, l), d_ff)
        outs_p.append((kp.reshape(bp, tp, heads, dh), vp.reshape(bp, tp, heads, dh),
                       mk.reshape(bp, n_mem, xa_heads, xa_dh), mv.reshape(bp, n_mem, xa_heads, xa_dh),
                       cp, pp, fp))
        outs_s.append((ks.reshape(bs, ts, heads, dh), vs.reshape(bs, ts, heads, dh), cs, ps, fs))
    y_prompt = rmsnorm(xp, g_final, F32).reshape(bp, tp, d)
    y_sample = rmsnorm(xs, g_final, F32).reshape(bs, ts, d)
    stack = lambda rows, i: jnp.stack([r[i] for r in rows])
    return ((y_prompt, y_sample) + tuple(stack(outs_p, i) for i in range(7))
            + tuple(stack(outs_s, i) for i in range(5)))
```
